```python
import math
import jax
import jax.numpy as jnp
from jax import lax
import numpy as np

D_MODEL = 1024
BATCH = 32
SEQ = 2048
DEPTH = 4

CONV_WIDTH = 256
CONV_GROUPS = 4
DW_CONV_LEN = 31
GDN_HEADS = 4
GDN_HEAD_DIM = 64
GDN_WIDTH = GDN_HEADS * GDN_HEAD_DIM
GDN_SHORT_CONV = 4
GDN_CHUNK = 64
DIFF_HEADS = 4
DIFF_QK_DIM = 64
DIFF_V_DIM = 2 * DIFF_QK_DIM
DIFF_WIDTH = DIFF_HEADS * DIFF_V_DIM
MIX_WIDTH = CONV_WIDTH + GDN_WIDTH + DIFF_WIDTH
ATTN_BLOCK = 128
D_FF = -(-(8 * D_MODEL) // (3 * 256)) * 256
RMS_EPS = 1e-6
LN_EPS = 1e-5

IN_CONV = 2 * CONV_WIDTH
IN_GDN_QKV = 3 * GDN_WIDTH
IN_GDN_Z = GDN_WIDTH
IN_GDN_AB = 2 * GDN_HEADS
IN_DIFF_QK = DIFF_HEADS * 2 * DIFF_QK_DIM
IN_DIFF_V = DIFF_WIDTH
SPLIT_1 = IN_CONV
SPLIT_2 = SPLIT_1 + IN_GDN_QKV
SPLIT_3 = SPLIT_2 + IN_GDN_Z
SPLIT_4 = SPLIT_3 + IN_GDN_AB
SPLIT_5 = SPLIT_4 + IN_DIFF_QK
SPLIT_6 = SPLIT_5 + IN_DIFF_QK
IN_WIDTH = SPLIT_6 + IN_DIFF_V

kernel_name = 'hymba_style_conv_gdn_diffattn_block'


def rms_norm(x, g):
    xf = x.astype(jnp.float32)
    y = xf * lax.rsqrt(jnp.mean(xf * xf, axis=-1, keepdims=True) + RMS_EPS)
    return (y * g.astype(jnp.float32)).astype(x.dtype)


def layer_norm(x, g, b):
    xf = x.astype(jnp.float32)
    mu = jnp.mean(xf, axis=-1, keepdims=True)
    xc = xf - mu
    var = jnp.mean(xc * xc, axis=-1, keepdims=True)
    y = xc * lax.rsqrt(var + LN_EPS) * g.astype(jnp.float32) + b.astype(jnp.float32)
    return y.astype(x.dtype)


def l2_normalize(x):
    return x * lax.rsqrt(jnp.sum(x * x, axis=-1, keepdims=True) + 1e-6)


def causal_depthwise_conv(x, w):
    k_len, ch = w.shape
    xp = jnp.pad(x, ((0, 0), (k_len - 1, 0), (0, 0)))
    return lax.conv_general_dilated(
        xp, w[:, None, :].astype(x.dtype), window_strides=(1,), padding='VALID',
        dimension_numbers=('NWC', 'WIO', 'NWC'), feature_group_count=ch)


def alibi_slopes(n):
    start = 2.0 ** (-8.0 / n)
    return jnp.asarray(np.array([start ** (i + 1) for i in range(n)], dtype=np.float32))


def conformer_conv_mixer(u, w_dw, b_dw, ln_g, ln_b):
    val, gate = jnp.split(u, 2, axis=-1)
    h = val * jax.nn.sigmoid(gate)
    h = causal_depthwise_conv(h, w_dw) + b_dw.astype(h.dtype)
    h = layer_norm(h, ln_g, ln_b)
    return jax.nn.silu(h)


def gated_delta_chunked(q, k, v, g, beta):
    bsz, seq, nh, dk = q.shape
    dv = v.shape[-1]
    c = GDN_CHUNK
    n = seq // c
    q = q * (dk ** -0.5)

    def chunks(t):
        return t.reshape(bsz, n, c, nh, -1).transpose(0, 3, 1, 2, 4)

    q, k, v = chunks(q), chunks(k), chunks(v)
    g = jnp.cumsum(g.reshape(bsz, n, c, nh).transpose(0, 3, 1, 2), axis=-1)
    beta = beta.reshape(bsz, n, c, nh).transpose(0, 3, 1, 2)
    idx = jnp.arange(c)
    causal = idx[:, None] >= idx[None, :]
    strict = idx[:, None] > idx[None, :]
    gdiff = g[..., :, None] - g[..., None, :]
    decay = jnp.where(causal, jnp.exp(jnp.where(causal, gdiff, 0.0)), 0.0)
    kk = jnp.einsum('bhncd,bhnsd->bhncs', k, k)
    lower = jnp.where(strict, beta[..., None] * kk * decay, 0.0)
    a_mat = lower + jnp.eye(c, dtype=jnp.float32)
    rhs = jnp.concatenate([v * beta[..., None], k * (beta * jnp.exp(g))[..., None]], axis=-1)
    sol = lax.linalg.triangular_solve(a_mat, rhs, left_side=True, lower=True, unit_diagonal=True)
    u_c, w_c = sol[..., :dv], sol[..., dv:]
    qk = jnp.einsum('bhncd,bhnsd->bhncs', q, k) * decay

    def step(state, inp):
        q_i, k_i, u_i, w_i, qk_i, g_i = inp
        v_new = u_i - jnp.einsum('bhcd,bhde->bhce', w_i, state)
        o_i = (jnp.einsum('bhcd,bhde->bhce', q_i * jnp.exp(g_i)[..., None], state)
               + jnp.einsum('bhcs,bhse->bhce', qk_i, v_new))
        g_last = g_i[..., -1:]
        state = (state * jnp.exp(g_last)[..., None]
                 + jnp.einsum('bhcd,bhce->bhde', k_i * jnp.exp(g_last - g_i)[..., None], v_new))
        return state, o_i

    xs = tuple(jnp.moveaxis(t, 2, 0) for t in (q, k, u_c, w_c, qk, g))
    state0 = jnp.zeros((bsz, nh, dk, dv), jnp.float32)
    _, o = lax.scan(step, state0, xs)
    return o.transpose(1, 0, 3, 2, 4).reshape(bsz, seq, nh, dv)


def gated_deltanet_mixer(qkv, z, ab, w_conv, a_log, dt_bias, norm_w):
    bsz, seq, _ = qkv.shape
    out_dtype = qkv.dtype
    qkv = jax.nn.silu(causal_depthwise_conv(qkv, w_conv)).astype(jnp.float32)
    q, k, v = jnp.split(qkv, 3, axis=-1)
    q = l2_normalize(q.reshape(bsz, seq, GDN_HEADS, GDN_HEAD_DIM))
    k = l2_normalize(k.reshape(bsz, seq, GDN_HEADS, GDN_HEAD_DIM))
    v = v.reshape(bsz, seq, GDN_HEADS, GDN_HEAD_DIM)
    a_in, b_in = jnp.split(ab.astype(jnp.float32), 2, axis=-1)
    beta = jax.nn.sigmoid(b_in)
    g = -jnp.exp(a_log.astype(jnp.float32)) * jax.nn.softplus(a_in + dt_bias.astype(jnp.float32))
    o = gated_delta_chunked(q, k, v, g, beta)
    o = rms_norm(o, norm_w) * jax.nn.silu(
        z.astype(jnp.float32).reshape(bsz, seq, GDN_HEADS, GDN_HEAD_DIM))
    return o.reshape(bsz, seq, GDN_WIDTH).astype(out_dtype)


def differential_attention_mixer(q, k, v, lam_vecs, lam_init, subln_w):
    bsz, seq, _ = q.shape
    nh = DIFF_HEADS
    q = q.reshape(bsz, seq, nh, 2, DIFF_QK_DIM)
    k = k.reshape(bsz, seq, nh, 2, DIFF_QK_DIM)
    v = v.reshape(bsz, seq, nh, DIFF_V_DIM)
    lv = lam_vecs.astype(jnp.float32)
    lam = jnp.exp(jnp.sum(lv[0] * lv[1])) - jnp.exp(jnp.sum(lv[2] * lv[3])) + lam_init
    slopes = alibi_slopes(nh)
    scale = DIFF_QK_DIM ** -0.5
    nblk = seq // ATTN_BLOCK
    q_blocks = q.reshape(bsz, nblk, ATTN_BLOCK, nh, 2, DIFF_QK_DIM).transpose(1, 0, 2, 3, 4, 5)
    kpos = jnp.arange(seq)

    def block(args):
        q_blk, i = args
        qpos = i * ATTN_BLOCK + jnp.arange(ATTN_BLOCK)
        s = jnp.einsum('bqhcd,bkhcd->bhcqk', q_blk, k).astype(jnp.float32) * scale
        dist = (qpos[:, None] - kpos[None, :]).astype(jnp.float32)
        s = s - slopes[None, :, None, None, None] * dist
        s = jnp.where(dist >= 0.0, s, -jnp.inf)
        p = jax.nn.softmax(s, axis=-1)
        p = p[:, :, 0] - lam * p[:, :, 1]
        return jnp.einsum('bhqk,bkhd->bqhd', p.astype(v.dtype), v)

    o = lax.map(block, (q_blocks, jnp.arange(nblk)))
    o = o.transpose(1, 0, 2, 3, 4).reshape(bsz, seq, nh, DIFF_V_DIM)
    o = rms_norm(o, subln_w) * (1.0 - lam_init)
    return o.reshape(bsz, seq, DIFF_WIDTH)


def setup_inputs(seed: int = 0) -> dict:
    key = jax.random.key(seed)
    ks = jax.random.split(key, 20)
    f32 = jnp.float32
    nl = DEPTH

    def nrm(k, shape, scale):
        return jax.random.normal(k, shape, f32) * scale

    out_scale = (2.0 * DEPTH) ** -0.5
    dt = jnp.exp(jax.random.uniform(ks[9], (nl, GDN_HEADS), f32, math.log(1e-3), math.log(1e-1)))
    return {
        'x': nrm(ks[0], (BATCH, SEQ, D_MODEL), 1.0),
        'norm1_g': 1.0 + nrm(ks[1], (nl, D_MODEL), 0.02),
        'w_in': nrm(ks[2], (nl, D_MODEL, IN_WIDTH), D_MODEL ** -0.5),
        'conv_dw_w': nrm(ks[3], (nl, DW_CONV_LEN, CONV_WIDTH), DW_CONV_LEN ** -0.5),
        'conv_dw_b': nrm(ks[4], (nl, CONV_WIDTH), 0.01),
        'conv_ln_g': 1.0 + nrm(ks[5], (nl, CONV_WIDTH), 0.02),
        'conv_ln_b': nrm(ks[6], (nl, CONV_WIDTH), 0.01),
        'gdn_conv_w': nrm(ks[7], (nl, GDN_SHORT_CONV, 3 * GDN_WIDTH), GDN_SHORT_CONV ** -0.5),
        'gdn_a_log': jnp.log(jax.random.uniform(ks[8], (nl, GDN_HEADS), f32, 1.0, 16.0)),
        'gdn_dt_bias': dt + jnp.log(-jnp.expm1(-dt)),
        'gdn_norm_w': 1.0 + nrm(ks[10], (nl, GDN_HEAD_DIM), 0.02),
        'diff_lambda': nrm(ks[11], (nl, 4, DIFF_QK_DIM), 0.1),
        'diff_subln_w': 1.0 + nrm(ks[12], (nl, DIFF_V_DIM), 0.02),
        'w_out': nrm(ks[13], (nl, MIX_WIDTH, D_MODEL), MIX_WIDTH ** -0.5 * out_scale),
        'norm2_g': 1.0 + nrm(ks[14], (nl, D_MODEL), 0.02),
        'w_ffn_in': nrm(ks[15], (nl, D_MODEL, 2 * D_FF), D_MODEL ** -0.5),
        'w_ffn_out': nrm(ks[16], (nl, D_FF, D_MODEL), D_FF ** -0.5 * out_scale),
        'final_norm_g': 1.0 + nrm(ks[17], (D_MODEL,), 0.02),
    }


def reference(x, norm1_g, w_in, conv_dw_w, conv_dw_b, conv_ln_g, conv_ln_b, gdn_conv_w,
              gdn_a_log, gdn_dt_bias, gdn_norm_w, diff_lambda, diff_subln_w, w_out,
              norm2_g, w_ffn_in, w_ffn_out, final_norm_g):
    for l in range(DEPTH):
        lam_init = 0.8 - 0.6 * math.exp(-0.3 * l)
        h = rms_norm(x, norm1_g[l])
        u = jnp.einsum('btd,de->bte', h, w_in[l])
        u_conv, u_qkv, u_z, u_ab, u_dq, u_dk, u_dv = jnp.split(
            u, [SPLIT_1, SPLIT_2, SPLIT_3, SPLIT_4, SPLIT_5, SPLIT_6], axis=-1)
        y_conv = conformer_conv_mixer(u_conv, conv_dw_w[l], conv_dw_b[l], conv_ln_g[l], conv_ln_b[l])
        y_gdn = gated_deltanet_mixer(u_qkv, u_z, u_ab, gdn_conv_w[l], gdn_a_log[l],
                                     gdn_dt_bias[l], gdn_norm_w[l])
        y_diff = differential_attention_mixer(u_dq, u_dk, u_dv, diff_lambda[l], lam_init,
                                              diff_subln_w[l])
        y = jnp.concatenate([y_conv, y_gdn, y_diff], axis=-1)
        x = x + jnp.einsum('bte,ed->btd', y, w_out[l])
        h = rms_norm(x, norm2_g[l])
        gate, up = jnp.split(jnp.einsum('btd,df->btf', h, w_ffn_in[l]), 2, axis=-1)
        x = x + jnp.einsum('btf,fd->btd', jax.nn.silu(gate) * up, w_ffn_out[l])
    return rms_norm(x, final_norm_g)
```

```python
import functools
import math

import jax
import jax.numpy as jnp
import numpy as np
from jax import lax
from jax.experimental import pallas as pl
from jax.experimental.pallas import tpu as pltpu

F32 = jnp.float32
BF16 = jnp.bfloat16

CONV_WIDTH = 256
DW_CONV_LEN = 31
GDN_HEADS = 4
GDN_HEAD_DIM = 64
GDN_WIDTH = GDN_HEADS * GDN_HEAD_DIM
GDN_SHORT_CONV = 4
GDN_CHUNK = 64
DIFF_HEADS = 4
DIFF_QK_DIM = 64
DIFF_V_DIM = 128
DIFF_WIDTH = DIFF_HEADS * DIFF_V_DIM
RMS_EPS = 1e-6
LN_EPS = 1e-5
L2_EPS = 1e-6

AB_START = 2 * CONV_WIDTH + 4 * GDN_WIDTH
AB_WIDTH = 2 * GDN_HEADS
COL_CONV_VAL = 0
COL_CONV_GATE = CONV_WIDTH
COL_GDN_Q = 2 * CONV_WIDTH
COL_GDN_K = COL_GDN_Q + GDN_WIDTH
COL_GDN_V = COL_GDN_K + GDN_WIDTH
COL_GDN_Z = COL_GDN_V + GDN_WIDTH
COL_DIFF_Q = COL_GDN_Z + GDN_WIDTH
COL_DIFF_K = COL_DIFF_Q + DIFF_WIDTH
COL_DIFF_V = COL_DIFF_K + DIFF_WIDTH
MAIN_WIDTH = COL_DIFF_V + DIFF_WIDTH

LANES = 128
VMEM_LIMIT_BYTES = 56 * 1024 * 1024


def _cparams(sem):
    return pltpu.CompilerParams(dimension_semantics=sem, vmem_limit_bytes=VMEM_LIMIT_BYTES)


def _const_spec(shape):
    nd = len(shape)
    return pl.BlockSpec(shape, lambda *_: (0,) * nd, pipeline_mode=pl.Buffered(1))


def _rms(x, g):
    return x * lax.rsqrt(jnp.mean(x * x, axis=-1, keepdims=True) + RMS_EPS) * g


IN_PROJ_NCHUNK = 512


def _in_proj_kernel(x_ref, g_ref, w_ref, wab_ref, u_ref, ab_ref):
    h = _rms(x_ref[...], g_ref[...]).astype(BF16)
    for c in range(MAIN_WIDTH // IN_PROJ_NCHUNK):
        sl = slice(c * IN_PROJ_NCHUNK, (c + 1) * IN_PROJ_NCHUNK)
        u_ref[:, sl] = jnp.dot(h, w_ref[:, sl], preferred_element_type=F32).astype(BF16)
    ab_ref[...] = jnp.dot(h, wab_ref[...], preferred_element_type=F32)


def _in_proj(x2, g, w_main, w_ab, tm):
    m, d = x2.shape
    return pl.pallas_call(
        _in_proj_kernel,
        out_shape=(jax.ShapeDtypeStruct((m, MAIN_WIDTH), BF16),
                   jax.ShapeDtypeStruct((m, LANES), F32)),
        grid=(m // tm,),
        in_specs=[pl.BlockSpec((tm, d), lambda i: (i, 0)),
                  _const_spec((1, d)),
                  _const_spec((d, MAIN_WIDTH)),
                  _const_spec((d, LANES))],
        out_specs=(pl.BlockSpec((tm, MAIN_WIDTH), lambda i: (i, 0)),
                   pl.BlockSpec((tm, LANES), lambda i: (i, 0))),
        compiler_params=_cparams(("parallel",)),
        name="in_proj",
    )(x2, g, w_main, w_ab)


CONV_PAD = 32
CONV_ROWS = 64


def _conv_kernel(val_ref, gate_ref, w_ref, b_ref, lg_ref, lb_ref, y_ref, hp_ref, *, seq):
    val = val_ref[...].astype(F32)
    gate = gate_ref[...].astype(F32)
    hp_ref[0:CONV_PAD, :] = jnp.zeros((CONV_PAD, CONV_WIDTH), F32)
    hp_ref[CONV_PAD:CONV_PAD + seq, :] = val * jax.nn.sigmoid(gate)
    bias = b_ref[...]
    lg = lg_ref[...]
    lb = lb_ref[...]

    def chunk(c, carry):
        t0 = pl.multiple_of(c * CONV_ROWS, CONV_ROWS)
        acc = jnp.zeros((CONV_ROWS, CONV_WIDTH), F32)
        win = hp_ref[pl.ds(t0, CONV_ROWS + CONV_PAD), :]
        for j in range(DW_CONV_LEN):
            off = CONV_PAD - (DW_CONV_LEN - 1) + j
            acc = acc + w_ref[j:j + 1, :] * win[off:off + CONV_ROWS, :]
        h = acc + bias
        mu = jnp.mean(h, axis=-1, keepdims=True)
        hc = h - mu
        var = jnp.mean(hc * hc, axis=-1, keepdims=True)
        y = hc * lax.rsqrt(var + LN_EPS) * lg + lb
        y_ref[pl.ds(t0, CONV_ROWS), :] = jax.nn.silu(y).astype(BF16)
        return carry

    lax.fori_loop(0, seq // CONV_ROWS, chunk, 0)


def _conv_mixer(u3, w_dw, b_dw, ln_g, ln_b):
    bsz, seq, _ = u3.shape
    cw = CONV_WIDTH
    return pl.pallas_call(
        functools.partial(_conv_kernel, seq=seq),
        out_shape=jax.ShapeDtypeStruct((bsz, seq, cw), BF16),
        grid=(bsz,),
        in_specs=[pl.BlockSpec((None, seq, cw), lambda b: (b, 0, COL_CONV_VAL // cw)),
                  pl.BlockSpec((None, seq, cw), lambda b: (b, 0, COL_CONV_GATE // cw)),
                  _const_spec((DW_CONV_LEN, cw)),
                  _const_spec((1, cw)), _const_spec((1, cw)), _const_spec((1, cw))],
        out_specs=pl.BlockSpec((None, seq, cw), lambda b: (b, 0, 0)),
        scratch_shapes=[pltpu.VMEM((CONV_PAD + seq, cw), F32)],
        compiler_params=_cparams(("parallel",)),
        name="conv_mixer",
    )(u3, u3, w_dw, b_dw, ln_g, ln_b)


GW = GDN_WIDTH
GC = GDN_CHUNK
GDN_PAD = 8
GDN_ROWS = 256


def _split2(x):
    hi = x.astype(BF16)
    lo = (x - hi.astype(F32)).astype(BF16)
    return hi, lo


def _split3(x):
    hi = x.astype(BF16)
    r = x - hi.astype(F32)
    mid = r.astype(BF16)
    lo = (r - mid.astype(F32)).astype(BF16)
    return hi, mid, lo


def _dot(a, b):
    return jnp.dot(a, b, preferred_element_type=F32)


def _dot_nt(a, b):
    return lax.dot_general(a, b, (((1,), (1,)), ((), ())), preferred_element_type=F32)


def _dot_exact_lhs(lhs_bf16, x):
    p0, p1, p2 = _split3(x)
    return _dot(lhs_bf16, p0) + _dot(lhs_bf16, p1) + _dot(lhs_bf16, p2)


def _dot_exact_rhs(x, rhs_bf16):
    p0, p1, p2 = _split3(x)
    return _dot(p0, rhs_bf16) + _dot(p1, rhs_bf16) + _dot(p2, rhs_bf16)


def _gdn_kernel(q_ref, k_ref, v_ref, z_ref, ab_ref, cwq_ref, cwk_ref, cwv_ref, alog_ref, dtb_ref,
                nw_ref, y_ref,
                xq_ref, xk_ref, xv_ref, beta_ref, g_ref, o_ref, s_ref, ones_ref, *, seq):
    r256 = lax.broadcasted_iota(jnp.int32, (GW, GW), 0)
    c256 = lax.broadcasted_iota(jnp.int32, (GW, GW), 1)
    bd_mask = (r256 // GDN_HEAD_DIM) == (c256 // GDN_HEAD_DIM)
    ones_ref[...] = jnp.where(bd_mask, 1.0, 0.0).astype(BF16)
    ra = lax.broadcasted_iota(jnp.int32, (LANES, GW), 0)
    ca = lax.broadcasted_iota(jnp.int32, (LANES, GW), 1)
    expand_a = jnp.where(ra == ca // GDN_HEAD_DIM, 1.0, 0.0).astype(BF16)
    expand_b = jnp.where(ra == GDN_HEADS + ca // GDN_HEAD_DIM, 1.0, 0.0).astype(BF16)
    ri = lax.broadcasted_iota(jnp.int32, (GC, GW), 0)
    cj = lax.broadcasted_iota(jnp.int32, (GC, GW), 1) % GC
    causal = ri >= cj
    strict = ri > cj
    upper = ri <= cj
    r64 = lax.broadcasted_iota(jnp.int32, (GC, GC), 0)
    c64 = lax.broadcasted_iota(jnp.int32, (GC, GC), 1)
    ltri = jnp.where(r64 >= c64, 1.0, 0.0).astype(BF16)
    ones64 = jnp.ones((GC, GC), BF16)

    zpad = jnp.zeros((GDN_PAD, GW), F32)
    for src, dst in ((q_ref, xq_ref), (k_ref, xk_ref), (v_ref, xv_ref)):
        dst[0:GDN_PAD, :] = zpad
        dst[GDN_PAD:GDN_PAD + seq, :] = src[...].astype(F32)
    a_scale = -jnp.exp(alog_ref[...])
    dtb = dtb_ref[...]

    def head_sums(x):
        hi, lo = _split2(x)
        return _dot(hi, ones_ref[...]) + _dot(lo, ones_ref[...])

    nblk = seq // GDN_ROWS

    def prep(i, carry):
        r0 = pl.multiple_of((nblk - 1 - i) * GDN_ROWS, GDN_ROWS)

        def conv_silu(x_ref, w_ref):
            acc = jnp.zeros((GDN_ROWS, GW), F32)
            win = x_ref[pl.ds(r0, GDN_ROWS + GDN_PAD), :]
            for j in range(GDN_SHORT_CONV):
                off = GDN_PAD - (GDN_SHORT_CONV - 1) + j
                acc = acc + w_ref[j:j + 1, :] * win[off:off + GDN_ROWS, :]
            return jax.nn.silu(acc)

        q = conv_silu(xq_ref, cwq_ref)
        k = conv_silu(xk_ref, cwk_ref)
        v = conv_silu(xv_ref, cwv_ref)
        q = q * lax.rsqrt(head_sums(q * q) + L2_EPS) * (GDN_HEAD_DIM ** -0.5)
        k = k * lax.rsqrt(head_sums(k * k) + L2_EPS)
        ab = ab_ref[pl.ds(r0, GDN_ROWS), :]
        a_in = _dot_exact_rhs(ab, expand_a)
        b_in = _dot_exact_rhs(ab, expand_b)
        beta_ref[pl.ds(r0, GDN_ROWS), :] = jax.nn.sigmoid(b_in)
        g_ref[pl.ds(r0, GDN_ROWS), :] = a_scale * jax.nn.softplus(a_in + dtb)
        o_ref[pl.ds(r0, GDN_ROWS), :] = q
        xk_ref[pl.ds(GDN_PAD + r0, GDN_ROWS), :] = k
        xv_ref[pl.ds(GDN_PAD + r0, GDN_ROWS), :] = v
        return carry

    lax.fori_loop(0, nblk, prep, 0)

    s_ref[...] = jnp.zeros((GW, GW), F32)

    def bdiag(xb):
        t = jnp.concatenate([xb, xb, xb, xb], axis=0)
        return jnp.where(bd_mask, t, jnp.zeros_like(t))

    def chunk(n, carry):
        t0 = pl.multiple_of(n * GC, GC)
        qs = o_ref[pl.ds(t0, GC), :]
        kh = xk_ref[pl.ds(GDN_PAD + t0, GC), :]
        vv = xv_ref[pl.ds(GDN_PAD + t0, GC), :]
        bt = beta_ref[pl.ds(t0, GC), :]
        gg = g_ref[pl.ds(t0, GC), :]

        gcum = _dot_exact_lhs(ltri, gg)
        gcum_t = _dot_exact_lhs(ones64, jnp.where(upper, gg, 0.0))
        decay = jnp.where(causal, jnp.exp(jnp.where(causal, gcum - gcum_t, 0.0)), 0.0)
        kb = kh.astype(BF16)
        kbd = bdiag(kb)
        kk = _dot_nt(kb, kbd)
        qk = _dot_nt(qs.astype(BF16), kbd) * decay
        p = -jnp.where(strict, bt * kk * decay, 0.0)
        eg = jnp.exp(gcum)
        rv = vv * bt
        rk = kh * (bt * eg)

        steps = int(math.log2(GC))
        for i in range(steps):
            ph, plo = _split2(p)
            pa = jnp.concatenate([ph, plo], axis=0)

            def apply(x):
                xh, xl = _split2(x)
                top = _dot(pa, bdiag(xh))
                return top[:GC] + top[GC:] + _dot(ph, bdiag(xl))

            rv = rv + apply(rv)
            rk = rk + apply(rk)
            if i + 1 < steps:
                p = apply(p)

        s = s_ref[...]
        sb = s.astype(BF16)
        v_new = rv - _dot(rk.astype(BF16), sb)
        vnb = v_new.astype(BF16)
        o = _dot((qs * eg).astype(BF16), sb) + _dot(qk.astype(BF16), bdiag(vnb))
        g_last = gcum[GC - 1:GC, :]
        kd = (kh * jnp.exp(g_last - gcum)).T.astype(BF16)
        upd = _dot(kd, vnb)
        s_ref[...] = s * jnp.exp(g_last) + jnp.where(bd_mask, upd, 0.0)
        o_ref[pl.ds(t0, GC), :] = o
        return carry

    lax.fori_loop(0, seq // GC, chunk, 0)

    nw = nw_ref[...]

    def post(r, carry):
        r0 = pl.multiple_of(r * GDN_ROWS, GDN_ROWS)
        o = o_ref[pl.ds(r0, GDN_ROWS), :]
        ms = head_sums(o * o) * (1.0 / GDN_HEAD_DIM)
        on = o * lax.rsqrt(ms + RMS_EPS) * nw
        z = z_ref[pl.ds(r0, GDN_ROWS), :].astype(F32)
        y_ref[pl.ds(r0, GDN_ROWS), :] = (on * jax.nn.silu(z)).astype(BF16)
        return carry

    lax.fori_loop(0, seq // GDN_ROWS, post, 0)


def _gdn_mixer(u3, ab3, conv_w, a_log, dt_bias, norm_w):
    bsz, seq, _ = u3.shape
    gw = GW
    cwq, cwk, cwv = conv_w[:, :gw], conv_w[:, gw:2 * gw], conv_w[:, 2 * gw:]
    alog = jnp.repeat(a_log, GDN_HEAD_DIM)[None, :]
    dtb = jnp.repeat(dt_bias, GDN_HEAD_DIM)[None, :]
    nw = jnp.tile(norm_w, GDN_HEADS)[None, :]
    col = lambda c: pl.BlockSpec((None, seq, gw), lambda b: (b, 0, c // gw))
    return pl.pallas_call(
        functools.partial(_gdn_kernel, seq=seq),
        out_shape=jax.ShapeDtypeStruct((bsz, seq, gw), BF16),
        grid=(bsz,),
        in_specs=[col(COL_GDN_Q), col(COL_GDN_K), col(COL_GDN_V), col(COL_GDN_Z),
                  pl.BlockSpec((None, seq, LANES), lambda b: (b, 0, 0)),
                  _const_spec((GDN_SHORT_CONV, gw)), _const_spec((GDN_SHORT_CONV, gw)),
                  _const_spec((GDN_SHORT_CONV, gw)),
                  _const_spec((1, gw)), _const_spec((1, gw)), _const_spec((1, gw))],
        out_specs=pl.BlockSpec((None, seq, gw), lambda b: (b, 0, 0)),
        scratch_shapes=[pltpu.VMEM((GDN_PAD + seq, gw), F32),
                        pltpu.VMEM((GDN_PAD + seq, gw), F32),
                        pltpu.VMEM((GDN_PAD + seq, gw), F32),
                        pltpu.VMEM((seq, gw), F32),
                        pltpu.VMEM((seq, gw), F32),
                        pltpu.VMEM((seq, gw), F32),
                        pltpu.VMEM((gw, gw), F32),
                        pltpu.VMEM((gw, gw), BF16)],
        compiler_params=_cparams(("parallel",)),
        name="gdn_mixer",
    )(u3, u3, u3, u3, ab3, cwq, cwk, cwv, alog, dtb, nw)


def _alibi_slopes(n):
    start = 2.0 ** (-8.0 / n)
    return np.array([start ** (i + 1) for i in range(n)], dtype=np.float32)


def _alibi_tables(seq):
    slopes = _alibi_slopes(DIFF_HEADS).astype(np.float64)
    t = np.arange(seq, dtype=np.int64)
    lo = (t % 256).astype(np.float64)
    hi = (t - t % 256).astype(np.float64)
    posq = np.zeros((DIFF_HEADS, seq, LANES), np.float32)
    posk = np.zeros((DIFF_HEADS, seq, LANES), np.float32)
    for h in range(DIFF_HEADS):
        posq[h, :, 0] = -slopes[h] * hi
        posq[h, :, 1] = -slopes[h] * lo
        posq[h, :, 2] = 1.0
        posq[h, :, 3] = 1.0
        posk[h, :, 0] = 1.0
        posk[h, :, 1] = 1.0
        posk[h, :, 2] = slopes[h] * hi
        posk[h, :, 3] = slopes[h] * lo
    pq, pk = posq.astype(BF16), posk.astype(BF16)
    assert np.array_equal(pq.astype(np.float32), posq), "ALiBi factors must be bf16-exact"
    assert np.array_equal(pk.astype(np.float32), posk), "ALiBi factors must be bf16-exact"
    return jnp.asarray(pq), jnp.asarray(pk)


def _attn_kernel(q_ref, k_ref, v_ref, pq_ref, pk_ref, lam_ref, w_ref, y_ref,
                 ka_ref, va_ref, m_ref, acc_ref, *, tq, lam_init):
    i = pl.program_id(2)
    dv = DIFF_V_DIM

    @pl.when(i == 0)
    def _():
        ka_ref[:, 0:LANES] = k_ref[...]
        ka_ref[:, LANES:2 * LANES] = pk_ref[...]
        va_ref[:, 0:dv] = v_ref[...]
        va_ref[:, dv:2 * dv] = jnp.ones(v_ref.shape, BF16)

    q = q_ref[...]
    lane = lax.broadcasted_iota(jnp.int32, q.shape, 1)
    scale = jnp.asarray(DIFF_QK_DIM ** -0.5, BF16)
    zero = jnp.zeros_like(q)
    q1 = jnp.where(lane < DIFF_QK_DIM, q, zero) * scale
    q2 = jnp.where(lane >= DIFF_QK_DIM, q, zero) * scale
    pq = pq_ref[...]
    qa = jnp.concatenate([jnp.concatenate([q1, pq], axis=1),
                          jnp.concatenate([q2, pq], axis=1)], axis=0)

    m_ref[...] = jnp.full(m_ref.shape, -jnp.inf, F32)
    acc_ref[...] = jnp.zeros(acc_ref.shape, F32)

    def tile(j, masked):
        k0 = pl.multiple_of(j * tq, tq)
        s = _dot_nt(qa, ka_ref[pl.ds(k0, tq), :])
        if masked:
            row = lax.broadcasted_iota(jnp.int32, s.shape, 0) % tq
            col = lax.broadcasted_iota(jnp.int32, s.shape, 1)
            s = jnp.where(col <= row, s, -jnp.inf)
        m_prev = m_ref[...]
        m_new = jnp.maximum(m_prev, jnp.max(s, axis=1, keepdims=True))
        alpha = jnp.exp(m_prev - m_new)
        p = jnp.exp(s - pltpu.repeat(m_new, tq // LANES, axis=1)).astype(BF16)
        acc_ref[...] = (acc_ref[...] * pltpu.repeat(alpha, 2 * dv // LANES, axis=1)
                        + _dot(p, va_ref[pl.ds(k0, tq), :]))
        m_ref[...] = m_new

    def body(j, carry):
        tile(j, False)
        return carry

    lax.fori_loop(0, i, body, 0)
    tile(i, True)

    lv = lam_ref[...]
    lam = (jnp.exp(jnp.sum(lv[0:1, :] * lv[1:2, :], axis=-1, keepdims=True))
           - jnp.exp(jnp.sum(lv[2:3, :] * lv[3:4, :], axis=-1, keepdims=True)) + lam_init)
    acc = acc_ref[...]
    o1 = acc[0:tq, 0:dv] / acc[0:tq, dv:2 * dv]
    o2 = acc[tq:2 * tq, 0:dv] / acc[tq:2 * tq, dv:2 * dv]
    o = o1 - lam * o2
    y_ref[...] = (_rms(o, w_ref[...]) * (1.0 - lam_init)).astype(BF16)


def _diff_attention(u3, lam_vecs, subln_w, lam_init, tq):
    bsz, seq, _ = u3.shape
    posq, posk = _alibi_tables(seq)
    dv = DIFF_V_DIM
    return pl.pallas_call(
        functools.partial(_attn_kernel, tq=tq, lam_init=lam_init),
        out_shape=jax.ShapeDtypeStruct((bsz, seq, DIFF_WIDTH), BF16),
        grid=(bsz, DIFF_HEADS, seq // tq),
        in_specs=[pl.BlockSpec((None, tq, dv), lambda b, h, i: (b, i, COL_DIFF_Q // dv + h)),
                  pl.BlockSpec((None, seq, dv), lambda b, h, i: (b, 0, COL_DIFF_K // dv + h)),
                  pl.BlockSpec((None, seq, dv), lambda b, h, i: (b, 0, COL_DIFF_V // dv + h)),
                  pl.BlockSpec((None, tq, LANES), lambda b, h, i: (h, i, 0)),
                  pl.BlockSpec((None, seq, LANES), lambda b, h, i: (h, 0, 0)),
                  pl.BlockSpec((4, DIFF_QK_DIM), lambda b, h, i: (0, 0)),
                  pl.BlockSpec((1, dv), lambda b, h, i: (0, 0))],
        out_specs=pl.BlockSpec((None, tq, dv), lambda b, h, i: (b, i, h)),
        scratch_shapes=[pltpu.VMEM((seq, 2 * LANES), BF16),
                        pltpu.VMEM((seq, 2 * dv), BF16),
                        pltpu.VMEM((2 * tq, LANES), F32),
                        pltpu.VMEM((2 * tq, 2 * dv), F32)],
        compiler_params=_cparams(("parallel", "parallel", "arbitrary")),
        name="diff_attention",
    )(u3, u3, u3, posq, posk, lam_vecs, subln_w)


FFN_CHUNK = 256


def _out_ffn_kernel(x_ref, yc_ref, yg_ref, yd_ref, wo_ref, g2_ref, wi_ref, wf_ref, gf_ref, o_ref,
                    acc_ref, *, d_ff, final):
    c0, c1 = CONV_WIDTH, CONV_WIDTH + GDN_WIDTH
    x1 = (x_ref[...]
          + _dot(yc_ref[...], wo_ref[0:c0, :])
          + _dot(yg_ref[...], wo_ref[c0:c1, :])
          + _dot(yd_ref[...], wo_ref[c1:, :]))
    h = _rms(x1, g2_ref[...]).astype(BF16)
    acc_ref[...] = x1

    def chunk(c, carry):
        lo = pl.multiple_of(c * FFN_CHUNK, FFN_CHUNK)
        gate = _dot(h, wi_ref[:, pl.ds(lo, FFN_CHUNK)])
        up = _dot(h, wi_ref[:, pl.ds(d_ff + lo, FFN_CHUNK)])
        a = (jax.nn.silu(gate) * up).astype(BF16)
        acc_ref[...] += _dot(a, wf_ref[pl.ds(lo, FFN_CHUNK), :])
        return carry

    lax.fori_loop(0, d_ff // FFN_CHUNK, chunk, 0)
    out = acc_ref[...]
    if final:
        out = _rms(out, gf_ref[...])
    o_ref[...] = out


def _out_ffn(x2, y_conv, y_gdn, y_diff, w_out, g2, w_ffn_in, w_ffn_out, g_final, final, tm):
    m, d = x2.shape
    d_ff = w_ffn_out.shape[0]
    row = lambda w: pl.BlockSpec((tm, w), lambda i: (i, 0))
    return pl.pallas_call(
        functools.partial(_out_ffn_kernel, d_ff=d_ff, final=final),
        out_shape=jax.ShapeDtypeStruct((m, d), F32),
        grid=(m // tm,),
        in_specs=[row(d), row(CONV_WIDTH), row(GDN_WIDTH), row(DIFF_WIDTH),
                  _const_spec(w_out.shape), _const_spec((1, d)),
                  _const_spec(w_ffn_in.shape), _const_spec(w_ffn_out.shape), _const_spec((1, d))],
        out_specs=row(d),
        scratch_shapes=[pltpu.VMEM((tm, d), F32)],
        compiler_params=_cparams(("parallel",)),
        name="out_ffn",
    )(x2, y_conv, y_gdn, y_diff, w_out, g2, w_ffn_in, w_ffn_out, g_final)


def _pick_tile(n, pref):
    t = min(pref, n)
    while n % t:
        t //= 2
    return t


def kernel(x, norm1_g, w_in, conv_dw_w, conv_dw_b, conv_ln_g, conv_ln_b, gdn_conv_w, gdn_a_log,
           gdn_dt_bias, gdn_norm_w, diff_lambda, diff_subln_w, w_out, norm2_g, w_ffn_in, w_ffn_out,
           final_norm_g):
    bsz, seq, d = x.shape
    depth = w_in.shape[0]
    m = bsz * seq
    assert seq % GDN_ROWS == 0 and seq % CONV_ROWS == 0
    tm = _pick_tile(m, 512)
    tq = _pick_tile(seq, 512)

    w_main = jnp.concatenate([w_in[..., :AB_START], w_in[..., AB_START + AB_WIDTH:]], axis=-1).astype(BF16)
    w_ab = jnp.pad(w_in[..., AB_START:AB_START + AB_WIDTH], ((0, 0), (0, 0), (0, LANES - AB_WIDTH))).astype(BF16)
    w_out_b = w_out.astype(BF16)
    w_ffn_in_b = w_ffn_in.astype(BF16)
    w_ffn_out_b = w_ffn_out.astype(BF16)

    x2 = x.reshape(m, d)
    for l in range(depth):
        lam_init = 0.8 - 0.6 * math.exp(-0.3 * l)
        u2, ab2 = _in_proj(x2, norm1_g[l][None, :], w_main[l], w_ab[l], tm)
        u3 = u2.reshape(bsz, seq, MAIN_WIDTH)
        ab3 = ab2.reshape(bsz, seq, LANES)
        y_conv = _conv_mixer(u3, conv_dw_w[l], conv_dw_b[l][None, :], conv_ln_g[l][None, :],
                             conv_ln_b[l][None, :])
        y_gdn = _gdn_mixer(u3, ab3, gdn_conv_w[l], gdn_a_log[l], gdn_dt_bias[l], gdn_norm_w[l])
        y_diff = _diff_attention(u3, diff_lambda[l], diff_subln_w[l][None, :], lam_init, tq)
        x2 = _out_ffn(x2, y_conv.reshape(m, CONV_WIDTH), y_gdn.reshape(m, GDN_WIDTH),
                      y_diff.reshape(m, DIFF_WIDTH), w_out_b[l], norm2_g[l][None, :],
                      w_ffn_in_b[l], w_ffn_out_b[l], final_norm_g[None, :],
                      final=(l == depth - 1), tm=tm)
    return x2.reshape(bsz, seq, d)
```

```python
import functools
import math

import jax
import jax.numpy as jnp
import numpy as np
from jax import lax
from jax.experimental import pallas as pl
from jax.experimental.pallas import tpu as pltpu

F32 = jnp.float32
BF16 = jnp.bfloat16

CONV_WIDTH = 256
DW_CONV_LEN = 31
GDN_HEADS = 4
GDN_HEAD_DIM = 64
GDN_WIDTH = GDN_HEADS * GDN_HEAD_DIM
GDN_SHORT_CONV = 4
GDN_CHUNK = 64
DIFF_HEADS = 4
DIFF_QK_DIM = 64
DIFF_V_DIM = 128
DIFF_WIDTH = DIFF_HEADS * DIFF_V_DIM
RMS_EPS = 1e-6
LN_EPS = 1e-5
L2_EPS = 1e-6

AB_START = 2 * CONV_WIDTH + 4 * GDN_WIDTH
AB_WIDTH = 2 * GDN_HEADS
COL_CONV_VAL = 0
COL_CONV_GATE = CONV_WIDTH
COL_GDN_Q = 2 * CONV_WIDTH
COL_GDN_K = COL_GDN_Q + GDN_WIDTH
COL_GDN_V = COL_GDN_K + GDN_WIDTH
COL_GDN_Z = COL_GDN_V + GDN_WIDTH
COL_DIFF_Q = COL_GDN_Z + GDN_WIDTH
COL_DIFF_K = COL_DIFF_Q + DIFF_WIDTH
COL_DIFF_V = COL_DIFF_K + DIFF_WIDTH
MAIN_WIDTH = COL_DIFF_V + DIFF_WIDTH

LANES = 128
SUBLANES = 8
VMEM_LIMIT_BYTES = 56 * 1024 * 1024


def _cparams(sem):
    return pltpu.CompilerParams(dimension_semantics=sem, vmem_limit_bytes=VMEM_LIMIT_BYTES)


def _const_spec(shape):
    nd = len(shape)
    return pl.BlockSpec(shape, lambda *_: (0,) * nd, pipeline_mode=pl.Buffered(1))


def _rms(x, g):
    return x * lax.rsqrt(jnp.mean(x * x, axis=-1, keepdims=True) + RMS_EPS) * g


def _in_proj_kernel(x_ref, g_ref, w_ref, u_ref, ab_ref):
    h = _rms(x_ref[...], g_ref[...]).astype(BF16)
    r = _dot(h, w_ref[...])
    u_ref[...] = r[:, 0:MAIN_WIDTH].astype(BF16)
    ab_ref[...] = r[:, MAIN_WIDTH:]


def _in_proj(x2, g, w_all, tm):
    m, d = x2.shape
    return pl.pallas_call(
        _in_proj_kernel,
        out_shape=(jax.ShapeDtypeStruct((m, MAIN_WIDTH), BF16),
                   jax.ShapeDtypeStruct((m, LANES), F32)),
        grid=(m // tm,),
        in_specs=[pl.BlockSpec((tm, d), lambda i: (i, 0)),
                  _const_spec((1, d)),
                  _const_spec((d, MAIN_WIDTH + LANES))],
        out_specs=(pl.BlockSpec((tm, MAIN_WIDTH), lambda i: (i, 0)),
                   pl.BlockSpec((tm, LANES), lambda i: (i, 0))),
        compiler_params=_cparams(("parallel",)),
        name="in_proj",
    )(x2, g, w_all)


CONV_PAD = 32
CONV_ROWS = 64
CONV_SHIFT_ROWS = CONV_ROWS + CONV_PAD - SUBLANES


def _conv_kernel(val_ref, gate_ref, w_ref, b_ref, lg_ref, lb_ref, y_ref, hp_ref, sh_ref, hc_ref, *, seq):
    val = val_ref[...].astype(F32)
    gate = gate_ref[...].astype(F32)
    hp_ref[0:CONV_PAD, :] = jnp.zeros((CONV_PAD, CONV_WIDTH), F32)
    hp_ref[CONV_PAD:CONV_PAD + seq, :] = val * jax.nn.sigmoid(gate)
    bias = b_ref[...]
    lg = lg_ref[...]
    lb = lb_ref[...]

    def norm_act(c):
        t0 = pl.multiple_of(c * CONV_ROWS, CONV_ROWS)
        h = hc_ref[pl.ds(t0, CONV_ROWS), :]
        mu = jnp.mean(h, axis=-1, keepdims=True)
        hc = h - mu
        var = jnp.mean(hc * hc, axis=-1, keepdims=True)
        y = hc * lax.rsqrt(var + LN_EPS) * lg + lb
        y_ref[pl.ds(t0, CONV_ROWS), :] = jax.nn.silu(y).astype(BF16)

    def conv(c):
        t0 = pl.multiple_of(c * CONV_ROWS, CONV_ROWS)
        acc = jnp.zeros((CONV_ROWS, CONV_WIDTH), F32)
        win = hp_ref[pl.ds(t0, CONV_ROWS + CONV_PAD), :]
        base = CONV_PAD - (DW_CONV_LEN - 1)
        for r in range(1, SUBLANES):
            sh_ref[r - 1] = win[r:r + CONV_SHIFT_ROWS, :]
        for r in range(SUBLANES):
            for off in range(r, CONV_PAD + 1, SUBLANES):
                j = off - base
                if 0 <= j < DW_CONV_LEN:
                    if r == 0:
                        tap = win[off:off + CONV_ROWS, :]
                    else:
                        tap = sh_ref[r - 1, off - r:off - r + CONV_ROWS, :]
                    acc = acc + w_ref[j:j + 1, :] * tap
        hc_ref[pl.ds(t0, CONV_ROWS), :] = acc + bias

    nchunk = seq // CONV_ROWS
    conv(0)

    def chunk(c, carry):
        norm_act(c - 1)
        conv(c)
        return carry

    lax.fori_loop(1, nchunk, chunk, 0)
    norm_act(nchunk - 1)


def _conv_mixer(u3, w_dw, b_dw, ln_g, ln_b):
    bsz, seq, _ = u3.shape
    cw = CONV_WIDTH
    return pl.pallas_call(
        functools.partial(_conv_kernel, seq=seq),
        out_shape=jax.ShapeDtypeStruct((bsz, seq, cw), BF16),
        grid=(bsz,),
        in_specs=[pl.BlockSpec((None, seq, cw), lambda b: (b, 0, COL_CONV_VAL // cw)),
                  pl.BlockSpec((None, seq, cw), lambda b: (b, 0, COL_CONV_GATE // cw)),
                  _const_spec((DW_CONV_LEN, cw)),
                  _const_spec((1, cw)), _const_spec((1, cw)), _const_spec((1, cw))],
        out_specs=pl.BlockSpec((None, seq, cw), lambda b: (b, 0, 0)),
        scratch_shapes=[pltpu.VMEM((CONV_PAD + seq, cw), F32),
                        pltpu.VMEM((SUBLANES - 1, CONV_SHIFT_ROWS, cw), F32),
                        pltpu.VMEM((seq, cw), F32)],
        compiler_params=_cparams(("parallel",)),
        name="conv_mixer",
    )(u3, u3, w_dw, b_dw, ln_g, ln_b)


GW = GDN_WIDTH
GC = GDN_CHUNK
GDN_PAD = 8
GDN_ROWS = 256
GDN_GROUP = 4
GDN_SPLIT_STEPS = 2


def _split2(x):
    hi = x.astype(BF16)
    lo = (x - hi.astype(F32)).astype(BF16)
    return hi, lo


def _split3(x):
    hi = x.astype(BF16)
    r = x - hi.astype(F32)
    mid = r.astype(BF16)
    lo = (r - mid.astype(F32)).astype(BF16)
    return hi, mid, lo


def _dot(a, b):
    return jnp.dot(a, b, preferred_element_type=F32)


def _dot_nt(a, b):
    return lax.dot_general(a, b, (((1,), (1,)), ((), ())), preferred_element_type=F32)


def _dot_exact_lhs(lhs_bf16, x):
    p0, p1, p2 = _split3(x)
    return _dot(lhs_bf16, p0) + _dot(lhs_bf16, p1) + _dot(lhs_bf16, p2)


def _dot_exact_rhs(x, rhs_bf16):
    p0, p1, p2 = _split3(x)
    return _dot(p0, rhs_bf16) + _dot(p1, rhs_bf16) + _dot(p2, rhs_bf16)


def _gdn_kernel(q_ref, k_ref, v_ref, z_ref, ab_ref, cwq_ref, cwk_ref, cwv_ref, alog_ref, dtb_ref,
                nw_ref, y_ref,
                xq_ref, xk_ref, xv_ref, beta_ref, g_ref, qs_ref, o_ref, s_ref, ones_ref, *, seq):
    r256 = lax.broadcasted_iota(jnp.int32, (GW, GW), 0)
    c256 = lax.broadcasted_iota(jnp.int32, (GW, GW), 1)
    bd_mask = (r256 // GDN_HEAD_DIM) == (c256 // GDN_HEAD_DIM)
    ones_ref[...] = jnp.where(bd_mask, 1.0, 0.0).astype(BF16)
    ra = lax.broadcasted_iota(jnp.int32, (LANES, GW), 0)
    ca = lax.broadcasted_iota(jnp.int32, (LANES, GW), 1)
    expand_a = jnp.where(ra == ca // GDN_HEAD_DIM, 1.0, 0.0).astype(BF16)
    expand_b = jnp.where(ra == GDN_HEADS + ca // GDN_HEAD_DIM, 1.0, 0.0).astype(BF16)
    ri = lax.broadcasted_iota(jnp.int32, (GC, GW), 0)
    cj = lax.broadcasted_iota(jnp.int32, (GC, GW), 1) % GC
    causal = ri >= cj
    strict = ri > cj
    upper = ri <= cj
    r64 = lax.broadcasted_iota(jnp.int32, (GC, GC), 0)
    c64 = lax.broadcasted_iota(jnp.int32, (GC, GC), 1)
    ltri = jnp.where(r64 >= c64, 1.0, 0.0).astype(BF16)
    ones64 = jnp.ones((GC, GC), BF16)

    zpad = jnp.zeros((GDN_PAD, GW), F32)
    for src, dst in ((q_ref, xq_ref), (k_ref, xk_ref), (v_ref, xv_ref)):
        dst[0:GDN_PAD, :] = zpad
        dst[GDN_PAD:GDN_PAD + seq, :] = src[...].astype(F32)
    a_scale = -jnp.exp(alog_ref[...])
    dtb = dtb_ref[...]

    def head_sums(x):
        hi, lo = _split2(x)
        return _dot(hi, ones_ref[...]) + _dot(lo, ones_ref[...])

    nblk = seq // GDN_ROWS

    def prep(i, carry):
        r0 = pl.multiple_of((nblk - 1 - i) * GDN_ROWS, GDN_ROWS)

        def conv_silu(x_ref, w_ref):
            acc = jnp.zeros((GDN_ROWS, GW), F32)
            win = x_ref[pl.ds(r0, GDN_ROWS + GDN_PAD), :]
            for j in range(GDN_SHORT_CONV):
                off = GDN_PAD - (GDN_SHORT_CONV - 1) + j
                acc = acc + w_ref[j:j + 1, :] * win[off:off + GDN_ROWS, :]
            return jax.nn.silu(acc)

        q = conv_silu(xq_ref, cwq_ref)
        k = conv_silu(xk_ref, cwk_ref)
        v = conv_silu(xv_ref, cwv_ref)
        q = q * lax.rsqrt(head_sums(q * q) + L2_EPS) * (GDN_HEAD_DIM ** -0.5)
        k = k * lax.rsqrt(head_sums(k * k) + L2_EPS)
        ab = ab_ref[pl.ds(r0, GDN_ROWS), :]
        a_in = _dot_exact_rhs(ab, expand_a)
        b_in = _dot_exact_rhs(ab, expand_b)
        beta_ref[pl.ds(r0, GDN_ROWS), :] = jax.nn.sigmoid(b_in)
        g_ref[pl.ds(r0, GDN_ROWS), :] = a_scale * jax.nn.softplus(a_in + dtb)
        qs_ref[pl.ds(r0, GDN_ROWS), :] = q
        xk_ref[pl.ds(GDN_PAD + r0, GDN_ROWS), :] = k
        xv_ref[pl.ds(GDN_PAD + r0, GDN_ROWS), :] = v
        return carry

    lax.fori_loop(0, nblk, prep, 0)

    grows = GDN_GROUP * GC

    def bdiag(xb):
        t = jnp.concatenate([xb, xb, xb, xb], axis=0)
        return jnp.where(bd_mask, t, jnp.zeros_like(t))

    def solve_load(m):
        r0 = pl.multiple_of(m * grows, grows)
        return (qs_ref[pl.ds(r0, grows), :], xk_ref[pl.ds(GDN_PAD + r0, grows), :],
                xv_ref[pl.ds(GDN_PAD + r0, grows), :], beta_ref[pl.ds(r0, grows), :],
                g_ref[pl.ds(r0, grows), :])

    def solve_setup(qs, kh, vv, bt, gg):
        gcum = _dot_exact_lhs(ltri, gg)
        gcum_t = _dot_exact_lhs(ones64, jnp.where(upper, gg, 0.0))
        decay = jnp.where(causal, jnp.exp(jnp.where(causal, gcum - gcum_t, 0.0)), 0.0)
        kb = kh.astype(BF16)
        kbd = bdiag(kb)
        kk = _dot_nt(kb, kbd)
        qk = _dot_nt(qs.astype(BF16), kbd) * decay
        p = -jnp.where(strict, bt * kk * decay, 0.0)
        rv = vv * bt
        rk = kh * (bt * jnp.exp(gcum))
        return p, rv, rk, qk, gcum

    def solve_step(i, last, p, rv, rk):
        if i < GDN_SPLIT_STEPS:
            ph, plo = _split2(p)
            pa = jnp.concatenate([ph, plo], axis=0)

            def apply(x):
                xh, xl = _split2(x)
                top = _dot(pa, bdiag(xh))
                return top[:GC] + top[GC:] + _dot(ph, bdiag(xl))
        else:
            ph = p.astype(BF16)

            def apply(x):
                return _dot(ph, bdiag(x.astype(BF16)))

        return (p if last else apply(p)), rv + apply(rv), rk + apply(rk)

    def solve_stages(m, loaded):
        st = [solve_setup(*(a[j * GC:(j + 1) * GC] for a in loaded)) for j in range(GDN_GROUP)]
        yield
        steps = int(math.log2(GC))
        prk = [s[:3] for s in st]
        for i in range(steps):
            prk = [solve_step(i, i + 1 == steps, *x) for x in prk]
            yield
        outs = [(x[1], x[2], s[3], s[4]) for x, s in zip(prk, st)]
        res = [jnp.concatenate([o[i] for o in outs], axis=0) for i in range(4)]
        r0 = pl.multiple_of(m * grows, grows)
        xv_ref[pl.ds(GDN_PAD + r0, grows), :] = res[0]
        beta_ref[pl.ds(r0, grows), :] = res[1]
        xq_ref[pl.ds(GDN_PAD + r0, grows), :] = res[2]
        g_ref[pl.ds(r0, grows), :] = res[3]

    def update_load(m):
        r0 = pl.multiple_of(m * grows, grows)
        return (qs_ref[pl.ds(r0, grows), :], xk_ref[pl.ds(GDN_PAD + r0, grows), :],
                xv_ref[pl.ds(GDN_PAD + r0, grows), :], beta_ref[pl.ds(r0, grows), :],
                xq_ref[pl.ds(GDN_PAD + r0, grows), :], g_ref[pl.ds(r0, grows), :])

    def update_stages(m, loaded):
        s = s_ref[...]
        outs = []
        for j in range(GDN_GROUP):
            qs, kh, u_c, w_c, qk, gcum = (a[j * GC:(j + 1) * GC] for a in loaded)
            sb = s.astype(BF16)
            ws = _dot(w_c.astype(BF16), sb)
            qes = _dot((qs * jnp.exp(gcum)).astype(BF16), sb)
            yield
            vnb = (u_c - ws).astype(BF16)
            outs.append(qes + _dot(qk.astype(BF16), bdiag(vnb)))
            g_last = gcum[GC - 1:GC, :]
            kd = (kh * jnp.exp(g_last - gcum)).T.astype(BF16)
            s = s * jnp.exp(g_last) + jnp.where(bd_mask, _dot(kd, vnb), 0.0)
            yield
        s_ref[...] = s
        r0 = pl.multiple_of(m * grows, grows)
        o_ref[pl.ds(r0, grows), :] = jnp.concatenate(outs, axis=0)

    def weave(*gens):
        live = list(gens)
        while live:
            for g in list(live):
                if next(g, "done") == "done":
                    live.remove(g)

    ngroup = seq // grows
    s_ref[...] = jnp.zeros((GW, GW), F32)
    weave(solve_stages(0, solve_load(0)))

    def group(m, carry):
        upd_in = update_load(m)
        sol_in = solve_load(m + 1)
        weave(update_stages(m, upd_in), solve_stages(m + 1, sol_in))
        return carry

    lax.fori_loop(0, ngroup - 1, group, 0)
    weave(update_stages(ngroup - 1, update_load(ngroup - 1)))

    nw = nw_ref[...]

    def post(r, carry):
        r0 = pl.multiple_of(r * GDN_ROWS, GDN_ROWS)
        o = o_ref[pl.ds(r0, GDN_ROWS), :]
        ms = head_sums(o * o) * (1.0 / GDN_HEAD_DIM)
        on = o * lax.rsqrt(ms + RMS_EPS) * nw
        z = z_ref[pl.ds(r0, GDN_ROWS), :].astype(F32)
        y_ref[pl.ds(r0, GDN_ROWS), :] = (on * jax.nn.silu(z)).astype(BF16)
        return carry

    lax.fori_loop(0, seq // GDN_ROWS, post, 0)


def _gdn_mixer(u3, ab3, conv_w, a_log, dt_bias, norm_w):
    bsz, seq, _ = u3.shape
    gw = GW
    cwq, cwk, cwv = conv_w[:, :gw], conv_w[:, gw:2 * gw], conv_w[:, 2 * gw:]
    alog = jnp.repeat(a_log, GDN_HEAD_DIM)[None, :]
    dtb = jnp.repeat(dt_bias, GDN_HEAD_DIM)[None, :]
    nw = jnp.tile(norm_w, GDN_HEADS)[None, :]
    col = lambda c: pl.BlockSpec((None, seq, gw), lambda b: (b, 0, c // gw))
    return pl.pallas_call(
        functools.partial(_gdn_kernel, seq=seq),
        out_shape=jax.ShapeDtypeStruct((bsz, seq, gw), BF16),
        grid=(bsz,),
        in_specs=[col(COL_GDN_Q), col(COL_GDN_K), col(COL_GDN_V), col(COL_GDN_Z),
                  pl.BlockSpec((None, seq, LANES), lambda b: (b, 0, 0)),
                  _const_spec((GDN_SHORT_CONV, gw)), _const_spec((GDN_SHORT_CONV, gw)),
                  _const_spec((GDN_SHORT_CONV, gw)),
                  _const_spec((1, gw)), _const_spec((1, gw)), _const_spec((1, gw))],
        out_specs=pl.BlockSpec((None, seq, gw), lambda b: (b, 0, 0)),
        scratch_shapes=[pltpu.VMEM((GDN_PAD + seq, gw), F32),
                        pltpu.VMEM((GDN_PAD + seq, gw), F32),
                        pltpu.VMEM((GDN_PAD + seq, gw), F32),
                        pltpu.VMEM((seq, gw), F32),
                        pltpu.VMEM((seq, gw), F32),
                        pltpu.VMEM((seq, gw), F32),
                        pltpu.VMEM((seq, gw), F32),
                        pltpu.VMEM((gw, gw), F32),
                        pltpu.VMEM((gw, gw), BF16)],
        compiler_params=_cparams(("parallel",)),
        name="gdn_mixer",
    )(u3, u3, u3, u3, ab3, cwq, cwk, cwv, alog, dtb, nw)


def _alibi_slopes(n):
    start = 2.0 ** (-8.0 / n)
    return np.array([start ** (i + 1) for i in range(n)], dtype=np.float32)


def _alibi_tables(seq):
    slopes = _alibi_slopes(DIFF_HEADS).astype(np.float64)
    t = np.arange(seq, dtype=np.int64)
    lo = (t % 256).astype(np.float64)
    hi = (t - t % 256).astype(np.float64)
    posq = np.zeros((DIFF_HEADS, seq, LANES), np.float32)
    posk = np.zeros((DIFF_HEADS, seq, LANES), np.float32)
    for h in range(DIFF_HEADS):
        posq[h, :, 0] = -slopes[h] * hi
        posq[h, :, 1] = -slopes[h] * lo
        posq[h, :, 2] = 1.0
        posq[h, :, 3] = 1.0
        posk[h, :, 0] = 1.0
        posk[h, :, 1] = 1.0
        posk[h, :, 2] = slopes[h] * hi
        posk[h, :, 3] = slopes[h] * lo
    pq, pk = posq.astype(BF16), posk.astype(BF16)
    assert np.array_equal(pq.astype(np.float32), posq), "ALiBi factors must be bf16-exact"
    assert np.array_equal(pk.astype(np.float32), posk), "ALiBi factors must be bf16-exact"
    return jnp.asarray(pq), jnp.asarray(pk)


def _attn_kernel(q_ref, k_ref, v_ref, pq_ref, pk_ref, lam_ref, w_ref, y_ref,
                 ka_ref, va_ref, m_ref, acc_ref, sa_ref, sb_ref, *, tq, lam_init):
    dv = DIFF_V_DIM
    seq = q_ref.shape[0]
    nq = seq // tq

    ka_ref[:, 0:LANES] = k_ref[...]
    ka_ref[:, LANES:2 * LANES] = pk_ref[...]
    va_ref[:, 0:dv] = v_ref[...]
    va_ref[:, dv:2 * dv] = jnp.ones(v_ref.shape, BF16)

    lv = lam_ref[...]
    lam = (jnp.exp(jnp.sum(lv[0:1, :] * lv[1:2, :], axis=-1, keepdims=True))
           - jnp.exp(jnp.sum(lv[2:3, :] * lv[3:4, :], axis=-1, keepdims=True)) + lam_init)
    scale = jnp.asarray(DIFF_QK_DIM ** -0.5, BF16)

    def scores(i, j):
        q = q_ref[i * tq:(i + 1) * tq, :]
        lane = lax.broadcasted_iota(jnp.int32, q.shape, 1)
        zero = jnp.zeros_like(q)
        q1 = jnp.where(lane < DIFF_QK_DIM, q, zero) * scale
        q2 = jnp.where(lane >= DIFF_QK_DIM, q, zero) * scale
        pq = pq_ref[i * tq:(i + 1) * tq, :]
        qa = jnp.concatenate([jnp.concatenate([q1, pq], axis=1),
                              jnp.concatenate([q2, pq], axis=1)], axis=0)
        return _dot_nt(qa, ka_ref[j * tq:(j + 1) * tq, :])

    def softmax_pv(s, i, j):
        if j == i:
            row = lax.broadcasted_iota(jnp.int32, s.shape, 0) % tq
            col = lax.broadcasted_iota(jnp.int32, s.shape, 1)
            s = jnp.where(col <= row, s, -jnp.inf)
        m_cur = jnp.max(s, axis=1, keepdims=True)
        if j == 0:
            m_new = jnp.broadcast_to(m_cur, m_ref.shape)
        else:
            m_prev = m_ref[...]
            m_new = jnp.maximum(m_prev, m_cur)
        p = jnp.exp(s - pltpu.repeat(m_new, tq // LANES, axis=1)).astype(BF16)
        acc = _dot(p, va_ref[j * tq:(j + 1) * tq, :])
        if j > 0:
            alpha = jnp.exp(m_prev - m_new)
            acc = acc_ref[...] * pltpu.repeat(alpha, 2 * dv // LANES, axis=1) + acc
        if j < i:
            acc_ref[...] = acc
            m_ref[...] = m_new
        else:
            o1 = acc[0:tq, 0:dv] / acc[0:tq, dv:2 * dv]
            o2 = acc[tq:2 * tq, 0:dv] / acc[tq:2 * tq, dv:2 * dv]
            o = o1 - lam * o2
            y_ref[i * tq:(i + 1) * tq, :] = (_rms(o, w_ref[...]) * (1.0 - lam_init)).astype(BF16)

    tiles = [(i, j) for i in range(nq) for j in range(i + 1)]
    bufs = (sa_ref, sb_ref)
    bufs[0][...] = scores(*tiles[0])
    for t, (i, j) in enumerate(tiles):
        if t + 1 < len(tiles):
            bufs[(t + 1) % 2][...] = scores(*tiles[t + 1])
        softmax_pv(bufs[t % 2][...], i, j)


def _diff_attention(u3, lam_vecs, subln_w, lam_init, tq):
    bsz, seq, _ = u3.shape
    posq, posk = _alibi_tables(seq)
    dv = DIFF_V_DIM
    col = lambda c: pl.BlockSpec((None, seq, dv), lambda b, h: (b, 0, c // dv + h))
    pos = pl.BlockSpec((None, seq, LANES), lambda b, h: (h, 0, 0))
    return pl.pallas_call(
        functools.partial(_attn_kernel, tq=tq, lam_init=lam_init),
        out_shape=jax.ShapeDtypeStruct((bsz, seq, DIFF_WIDTH), BF16),
        grid=(bsz, DIFF_HEADS),
        in_specs=[col(COL_DIFF_Q), col(COL_DIFF_K), col(COL_DIFF_V), pos, pos,
                  pl.BlockSpec((4, DIFF_QK_DIM), lambda b, h: (0, 0)),
                  pl.BlockSpec((1, dv), lambda b, h: (0, 0))],
        out_specs=pl.BlockSpec((None, seq, dv), lambda b, h: (b, 0, h)),
        scratch_shapes=[pltpu.VMEM((seq, 2 * LANES), BF16),
                        pltpu.VMEM((seq, 2 * dv), BF16),
                        pltpu.VMEM((2 * tq, LANES), F32),
                        pltpu.VMEM((2 * tq, 2 * dv), F32),
                        pltpu.VMEM((2 * tq, tq), F32),
                        pltpu.VMEM((2 * tq, tq), F32)],
        compiler_params=_cparams(("parallel", "parallel")),
        name="diff_attention",
    )(u3, u3, u3, posq, posk, lam_vecs, subln_w)


def _out_ffn_kernel(x_ref, yc_ref, yg_ref, yd_ref, wo_ref, g2_ref, wi_ref, wf_ref, gf_ref, o_ref,
                    *, d_ff, final):
    c0, c1 = CONV_WIDTH, CONV_WIDTH + GDN_WIDTH
    x1 = (x_ref[...]
          + _dot(yc_ref[...], wo_ref[0:c0, :])
          + _dot(yg_ref[...], wo_ref[c0:c1, :])
          + _dot(yd_ref[...], wo_ref[c1:, :]))
    h = _rms(x1, g2_ref[...]).astype(BF16)
    gate = _dot(h, wi_ref[:, 0:d_ff])
    up = _dot(h, wi_ref[:, d_ff:2 * d_ff])
    a = (jax.nn.silu(gate) * up).astype(BF16)
    out = x1 + _dot(a, wf_ref[...])
    if final:
        out = _rms(out, gf_ref[...])
    o_ref[...] = out


def _out_ffn(x2, y_conv, y_gdn, y_diff, w_out, g2, w_ffn_in, w_ffn_out, g_final, final, tm):
    m, d = x2.shape
    d_ff = w_ffn_out.shape[0]
    row = lambda w: pl.BlockSpec((tm, w), lambda i: (i, 0))
    return pl.pallas_call(
        functools.partial(_out_ffn_kernel, d_ff=d_ff, final=final),
        out_shape=jax.ShapeDtypeStruct((m, d), F32),
        grid=(m // tm,),
        in_specs=[row(d), row(CONV_WIDTH), row(GDN_WIDTH), row(DIFF_WIDTH),
                  _const_spec(w_out.shape), _const_spec((1, d)),
                  _const_spec(w_ffn_in.shape), _const_spec(w_ffn_out.shape), _const_spec((1, d))],
        out_specs=row(d),
        compiler_params=_cparams(("parallel",)),
        name="out_ffn",
    )(x2, y_conv, y_gdn, y_diff, w_out, g2, w_ffn_in, w_ffn_out, g_final)


def _pick_tile(n, pref):
    t = min(pref, n)
    while n % t:
        t //= 2
    return t


def kernel(x, norm1_g, w_in, conv_dw_w, conv_dw_b, conv_ln_g, conv_ln_b, gdn_conv_w, gdn_a_log,
           gdn_dt_bias, gdn_norm_w, diff_lambda, diff_subln_w, w_out, norm2_g, w_ffn_in, w_ffn_out,
           final_norm_g):
    bsz, seq, d = x.shape
    depth = w_in.shape[0]
    m = bsz * seq
    assert seq % GDN_ROWS == 0 and seq % CONV_ROWS == 0
    tm = _pick_tile(m, 512)
    tq = _pick_tile(seq, 512)

    w_all = jnp.concatenate(
        [w_in[..., :AB_START], w_in[..., AB_START + AB_WIDTH:], w_in[..., AB_START:AB_START + AB_WIDTH],
         jnp.zeros(w_in.shape[:2] + (LANES - AB_WIDTH,), w_in.dtype)], axis=-1).astype(BF16)
    w_out_b = w_out.astype(BF16)
    w_ffn_in_b = w_ffn_in.astype(BF16)
    w_ffn_out_b = w_ffn_out.astype(BF16)

    x2 = x.reshape(m, d)
    for l in range(depth):
        lam_init = 0.8 - 0.6 * math.exp(-0.3 * l)
        u2, ab2 = _in_proj(x2, norm1_g[l][None, :], w_all[l], tm)
        u3 = u2.reshape(bsz, seq, MAIN_WIDTH)
        ab3 = ab2.reshape(bsz, seq, LANES)
        y_conv = _conv_mixer(u3, conv_dw_w[l], conv_dw_b[l][None, :], conv_ln_g[l][None, :],
                             conv_ln_b[l][None, :])
        y_gdn = _gdn_mixer(u3, ab3, gdn_conv_w[l], gdn_a_log[l], gdn_dt_bias[l], gdn_norm_w[l])
        y_diff = _diff_attention(u3, diff_lambda[l], diff_subln_w[l][None, :], lam_init, tq)
        x2 = _out_ffn(x2, y_conv.reshape(m, CONV_WIDTH), y_gdn.reshape(m, GDN_WIDTH),
                      y_diff.reshape(m, DIFF_WIDTH), w_out_b[l], norm2_g[l][None, :],
                      w_ffn_in_b[l], w_ffn_out_b[l], final_norm_g[None, :],
                      final=(l == depth - 1), tm=tm)
    return x2.reshape(bsz, seq, d)
```

```python
import functools
import math

import jax
import jax.numpy as jnp
import numpy as np
from jax import lax
from jax.experimental import pallas as pl
from jax.experimental.pallas import tpu as pltpu

F32 = jnp.float32
BF16 = jnp.bfloat16

CONV_WIDTH = 256
DW_CONV_LEN = 31
GDN_HEADS = 4
GDN_HEAD_DIM = 64
GDN_WIDTH = GDN_HEADS * GDN_HEAD_DIM
GDN_SHORT_CONV = 4
GDN_CHUNK = 64
DIFF_HEADS = 4
DIFF_QK_DIM = 64
DIFF_V_DIM = 128
DIFF_WIDTH = DIFF_HEADS * DIFF_V_DIM
RMS_EPS = 1e-6
LN_EPS = 1e-5
L2_EPS = 1e-6

AB_START = 2 * CONV_WIDTH + 4 * GDN_WIDTH
AB_WIDTH = 2 * GDN_HEADS
COL_CONV_VAL = 0
COL_CONV_GATE = CONV_WIDTH
COL_GDN_Q = 2 * CONV_WIDTH
COL_GDN_K = COL_GDN_Q + GDN_WIDTH
COL_GDN_V = COL_GDN_K + GDN_WIDTH
COL_GDN_Z = COL_GDN_V + GDN_WIDTH
COL_DIFF_Q = COL_GDN_Z + GDN_WIDTH
COL_DIFF_K = COL_DIFF_Q + DIFF_WIDTH
COL_DIFF_V = COL_DIFF_K + DIFF_WIDTH
MAIN_WIDTH = COL_DIFF_V + DIFF_WIDTH

LANES = 128
SUBLANES = 8
VMEM_LIMIT_BYTES = 56 * 1024 * 1024


def _cparams(sem):
    return pltpu.CompilerParams(dimension_semantics=sem, vmem_limit_bytes=VMEM_LIMIT_BYTES)


def _const_spec(shape):
    nd = len(shape)
    return pl.BlockSpec(shape, lambda *_: (0,) * nd, pipeline_mode=pl.Buffered(1))


def _rms(x, g):
    return x * lax.rsqrt(jnp.mean(x * x, axis=-1, keepdims=True) + RMS_EPS) * g


def _in_proj_kernel(x_ref, g_ref, w_ref, u_ref, ab_ref):
    h = _rms(x_ref[...], g_ref[...]).astype(BF16)
    r = _dot(h, w_ref[...])
    u_ref[...] = r[:, 0:MAIN_WIDTH].astype(BF16)
    ab_ref[...] = r[:, MAIN_WIDTH:]


def _in_proj(x2, g, w_all, tm):
    m, d = x2.shape
    return pl.pallas_call(
        _in_proj_kernel,
        out_shape=(jax.ShapeDtypeStruct((m, MAIN_WIDTH), BF16),
                   jax.ShapeDtypeStruct((m, LANES), F32)),
        grid=(m // tm,),
        in_specs=[pl.BlockSpec((tm, d), lambda i: (i, 0)),
                  _const_spec((1, d)),
                  _const_spec((d, MAIN_WIDTH + LANES))],
        out_specs=(pl.BlockSpec((tm, MAIN_WIDTH), lambda i: (i, 0)),
                   pl.BlockSpec((tm, LANES), lambda i: (i, 0))),
        compiler_params=_cparams(("parallel",)),
        name="in_proj",
    )(x2, g, w_all)


CONV_PAD = 32
CONV_ROWS = 64
CONV_SHIFT_ROWS = CONV_ROWS + CONV_PAD - SUBLANES


def _conv_kernel(val_ref, gate_ref, w_ref, b_ref, lg_ref, lb_ref, y_ref, hp_ref, sh_ref, hc_ref, *, seq):
    val = val_ref[...].astype(F32)
    gate = gate_ref[...].astype(F32)
    hp_ref[0:CONV_PAD, :] = jnp.zeros((CONV_PAD, CONV_WIDTH), F32)
    hp_ref[CONV_PAD:CONV_PAD + seq, :] = val * jax.nn.sigmoid(gate)
    bias = b_ref[...]
    lg = lg_ref[...]
    lb = lb_ref[...]

    def norm_act(c):
        t0 = pl.multiple_of(c * CONV_ROWS, CONV_ROWS)
        h = hc_ref[pl.ds(t0, CONV_ROWS), :]
        mu = jnp.mean(h, axis=-1, keepdims=True)
        hc = h - mu
        var = jnp.mean(hc * hc, axis=-1, keepdims=True)
        y = hc * lax.rsqrt(var + LN_EPS) * lg + lb
        y_ref[pl.ds(t0, CONV_ROWS), :] = jax.nn.silu(y).astype(BF16)

    def conv(c):
        t0 = pl.multiple_of(c * CONV_ROWS, CONV_ROWS)
        acc = jnp.zeros((CONV_ROWS, CONV_WIDTH), F32)
        win = hp_ref[pl.ds(t0, CONV_ROWS + CONV_PAD), :]
        base = CONV_PAD - (DW_CONV_LEN - 1)
        for r in range(1, SUBLANES):
            sh_ref[r - 1] = win[r:r + CONV_SHIFT_ROWS, :]
        for r in range(SUBLANES):
            for off in range(r, CONV_PAD + 1, SUBLANES):
                j = off - base
                if 0 <= j < DW_CONV_LEN:
                    if r == 0:
                        tap = win[off:off + CONV_ROWS, :]
                    else:
                        tap = sh_ref[r - 1, off - r:off - r + CONV_ROWS, :]
                    acc = acc + w_ref[j:j + 1, :] * tap
        hc_ref[pl.ds(t0, CONV_ROWS), :] = acc + bias

    nchunk = seq // CONV_ROWS
    conv(0)

    def chunk(c, carry):
        norm_act(c - 1)
        conv(c)
        return carry

    lax.fori_loop(1, nchunk, chunk, 0)
    norm_act(nchunk - 1)


def _conv_mixer(u3, w_dw, b_dw, ln_g, ln_b):
    bsz, seq, _ = u3.shape
    cw = CONV_WIDTH
    return pl.pallas_call(
        functools.partial(_conv_kernel, seq=seq),
        out_shape=jax.ShapeDtypeStruct((bsz, seq, cw), BF16),
        grid=(bsz,),
        in_specs=[pl.BlockSpec((None, seq, cw), lambda b: (b, 0, COL_CONV_VAL // cw)),
                  pl.BlockSpec((None, seq, cw), lambda b: (b, 0, COL_CONV_GATE // cw)),
                  _const_spec((DW_CONV_LEN, cw)),
                  _const_spec((1, cw)), _const_spec((1, cw)), _const_spec((1, cw))],
        out_specs=pl.BlockSpec((None, seq, cw), lambda b: (b, 0, 0)),
        scratch_shapes=[pltpu.VMEM((CONV_PAD + seq, cw), F32),
                        pltpu.VMEM((SUBLANES - 1, CONV_SHIFT_ROWS, cw), F32),
                        pltpu.VMEM((seq, cw), F32)],
        compiler_params=_cparams(("parallel",)),
        name="conv_mixer",
    )(u3, u3, w_dw, b_dw, ln_g, ln_b)


GW = GDN_WIDTH
GC = GDN_CHUNK
GDN_PAD = 8
GDN_ROWS = 256
GDN_GROUP = 4


def _split2(x):
    hi = x.astype(BF16)
    lo = (x - hi.astype(F32)).astype(BF16)
    return hi, lo


def _split3(x):
    hi = x.astype(BF16)
    r = x - hi.astype(F32)
    mid = r.astype(BF16)
    lo = (r - mid.astype(F32)).astype(BF16)
    return hi, mid, lo


def _dot(a, b):
    return jnp.dot(a, b, preferred_element_type=F32)


def _dot_nt(a, b):
    return lax.dot_general(a, b, (((1,), (1,)), ((), ())), preferred_element_type=F32)


def _dot_exact_lhs(lhs_bf16, x):
    p0, p1, p2 = _split3(x)
    return _dot(lhs_bf16, p0) + _dot(lhs_bf16, p1) + _dot(lhs_bf16, p2)


def _dot_exact_rhs(x, rhs_bf16):
    p0, p1, p2 = _split3(x)
    return _dot(p0, rhs_bf16) + _dot(p1, rhs_bf16) + _dot(p2, rhs_bf16)


def _gdn_kernel(q_ref, k_ref, v_ref, z_ref, ab_ref, cwq_ref, cwk_ref, cwv_ref, alog_ref, dtb_ref,
                nw_ref, y_ref,
                xq_ref, xk_ref, xv_ref, beta_ref, g_ref, qs_ref, o_ref, s_ref, ones_ref, *, seq):
    r256 = lax.broadcasted_iota(jnp.int32, (GW, GW), 0)
    c256 = lax.broadcasted_iota(jnp.int32, (GW, GW), 1)
    bd_mask = (r256 // GDN_HEAD_DIM) == (c256 // GDN_HEAD_DIM)
    ones_ref[...] = jnp.where(bd_mask, 1.0, 0.0).astype(BF16)
    ra = lax.broadcasted_iota(jnp.int32, (LANES, GW), 0)
    ca = lax.broadcasted_iota(jnp.int32, (LANES, GW), 1)
    expand_a = jnp.where(ra == ca // GDN_HEAD_DIM, 1.0, 0.0).astype(BF16)
    expand_b = jnp.where(ra == GDN_HEADS + ca // GDN_HEAD_DIM, 1.0, 0.0).astype(BF16)
    ri = lax.broadcasted_iota(jnp.int32, (GC, GW), 0)
    cj = lax.broadcasted_iota(jnp.int32, (GC, GW), 1) % GC
    causal = ri >= cj
    strict = ri > cj
    upper = ri <= cj
    r64 = lax.broadcasted_iota(jnp.int32, (GC, GC), 0)
    c64 = lax.broadcasted_iota(jnp.int32, (GC, GC), 1)
    ltri = jnp.where(r64 >= c64, 1.0, 0.0).astype(BF16)
    ones64 = jnp.ones((GC, GC), BF16)

    zpad = jnp.zeros((GDN_PAD, GW), F32)
    for src, dst in ((q_ref, xq_ref), (k_ref, xk_ref), (v_ref, xv_ref)):
        dst[0:GDN_PAD, :] = zpad
        dst[GDN_PAD:GDN_PAD + seq, :] = src[...].astype(F32)
    a_scale = -jnp.exp(alog_ref[...])
    dtb = dtb_ref[...]

    def head_sums(x):
        return _dot(x.astype(BF16), ones_ref[...])

    nblk = seq // GDN_ROWS

    def prep(i, carry):
        r0 = pl.multiple_of((nblk - 1 - i) * GDN_ROWS, GDN_ROWS)

        def conv_silu(x_ref, w_ref):
            acc = jnp.zeros((GDN_ROWS, GW), F32)
            win = x_ref[pl.ds(r0, GDN_ROWS + GDN_PAD), :]
            for j in range(GDN_SHORT_CONV):
                off = GDN_PAD - (GDN_SHORT_CONV - 1) + j
                tap = win[off:] if off == GDN_PAD else pltpu.roll(win, GDN_ROWS + GDN_PAD - off, 0)[:GDN_ROWS]
                acc = acc + w_ref[j:j + 1, :] * tap
            return jax.nn.silu(acc)

        q = conv_silu(xq_ref, cwq_ref)
        k = conv_silu(xk_ref, cwk_ref)
        v = conv_silu(xv_ref, cwv_ref)
        q = q * lax.rsqrt(head_sums(q * q) + L2_EPS) * (GDN_HEAD_DIM ** -0.5)
        k = k * lax.rsqrt(head_sums(k * k) + L2_EPS)
        ab = ab_ref[pl.ds(r0, GDN_ROWS), :]
        g_ref[pl.ds(r0, GDN_ROWS), :] = _dot_exact_rhs(a_scale * jax.nn.softplus(ab + dtb), expand_a)
        beta_ref[pl.ds(r0, GDN_ROWS), :] = _dot_exact_rhs(jax.nn.sigmoid(ab), expand_b)
        qs_ref[pl.ds(r0, GDN_ROWS), :] = q
        xk_ref[pl.ds(GDN_PAD + r0, GDN_ROWS), :] = k
        xv_ref[pl.ds(GDN_PAD + r0, GDN_ROWS), :] = v
        return carry

    lax.fori_loop(0, nblk, prep, 0)

    grows = GDN_GROUP * GC

    def bdiag(xb):
        t = jnp.concatenate([xb, xb, xb, xb], axis=0)
        return jnp.where(bd_mask, t, jnp.zeros_like(t))

    def solve_load(m):
        r0 = pl.multiple_of(m * grows, grows)
        return (qs_ref[pl.ds(r0, grows), :], xk_ref[pl.ds(GDN_PAD + r0, grows), :],
                xv_ref[pl.ds(GDN_PAD + r0, grows), :], beta_ref[pl.ds(r0, grows), :],
                g_ref[pl.ds(r0, grows), :])

    def solve_setup(qs, kh, vv, bt, gg):
        gcum = _dot_exact_lhs(ltri, gg)
        gcum_t = _dot_exact_lhs(ones64, jnp.where(upper, gg, 0.0))
        decay = jnp.where(causal, jnp.exp(jnp.where(causal, gcum - gcum_t, 0.0)), 0.0)
        kb = kh.astype(BF16)
        kq = _dot_nt(jnp.concatenate([kb, qs.astype(BF16)], axis=0), bdiag(kb))
        kk = kq[:GC]
        qk = kq[GC:] * decay
        p = -jnp.where(strict, bt * kk * decay, 0.0)
        rv = vv * bt
        rk = kh * (bt * jnp.exp(gcum))
        return p, rv, rk, qk, gcum

    def solve_step(last, p, rv, rk):
        ph = p.astype(BF16)

        def apply(x):
            return _dot(ph, bdiag(x.astype(BF16)))

        return (p if last else apply(p)), rv + apply(rv), rk + apply(rk)

    def solve_stages(m, loaded):
        st = [solve_setup(*(a[j * GC:(j + 1) * GC] for a in loaded)) for j in range(GDN_GROUP)]
        yield
        steps = int(math.log2(GC))
        prk = [s[:3] for s in st]
        for i in range(steps):
            prk = [solve_step(i + 1 == steps, *x) for x in prk]
            yield
        outs = [(x[1], x[2], s[3], s[4]) for x, s in zip(prk, st)]
        res = [jnp.concatenate([o[i] for o in outs], axis=0) for i in range(4)]
        r0 = pl.multiple_of(m * grows, grows)
        xv_ref[pl.ds(GDN_PAD + r0, grows), :] = res[0]
        beta_ref[pl.ds(r0, grows), :] = res[1]
        xq_ref[pl.ds(GDN_PAD + r0, grows), :] = res[2]
        g_ref[pl.ds(r0, grows), :] = res[3]

    def update_load(m):
        r0 = pl.multiple_of(m * grows, grows)
        return (qs_ref[pl.ds(r0, grows), :], xk_ref[pl.ds(GDN_PAD + r0, grows), :],
                xv_ref[pl.ds(GDN_PAD + r0, grows), :], beta_ref[pl.ds(r0, grows), :],
                xq_ref[pl.ds(GDN_PAD + r0, grows), :], g_ref[pl.ds(r0, grows), :])

    def update_stages(m, loaded):
        s = s_ref[...]
        outs = []
        for j in range(GDN_GROUP):
            qs, kh, u_c, w_c, qk, gcum = (a[j * GC:(j + 1) * GC] for a in loaded)
            wq = _dot(jnp.concatenate([w_c.astype(BF16), (qs * jnp.exp(gcum)).astype(BF16)], axis=0),
                      s.astype(BF16))
            ws, qes = wq[:GC], wq[GC:]
            yield
            vnb = (u_c - ws).astype(BF16)
            outs.append(qes + _dot(qk.astype(BF16), bdiag(vnb)))
            g_last = gcum[GC - 1:GC, :]
            kd = (kh * jnp.exp(g_last - gcum)).T.astype(BF16)
            s = s * jnp.exp(g_last) + jnp.where(bd_mask, _dot(kd, vnb), 0.0)
            yield
        s_ref[...] = s
        r0 = pl.multiple_of(m * grows, grows)
        o_ref[pl.ds(r0, grows), :] = jnp.concatenate(outs, axis=0)

    def weave(*gens):
        live = list(gens)
        while live:
            for g in list(live):
                if next(g, "done") == "done":
                    live.remove(g)

    ngroup = seq // grows
    s_ref[...] = jnp.zeros((GW, GW), F32)
    weave(solve_stages(0, solve_load(0)))

    def group(m, carry):
        upd_in = update_load(m)
        sol_in = solve_load(m + 1)
        weave(update_stages(m, upd_in), solve_stages(m + 1, sol_in))
        return carry

    lax.fori_loop(0, ngroup - 1, group, 0)
    weave(update_stages(ngroup - 1, update_load(ngroup - 1)))

    nw = nw_ref[...]

    def post(r, carry):
        r0 = pl.multiple_of(r * GDN_ROWS, GDN_ROWS)
        o = o_ref[pl.ds(r0, GDN_ROWS), :]
        ms = head_sums(o * o) * (1.0 / GDN_HEAD_DIM)
        on = o * lax.rsqrt(ms + RMS_EPS) * nw
        z = z_ref[pl.ds(r0, GDN_ROWS), :].astype(F32)
        y_ref[pl.ds(r0, GDN_ROWS), :] = (on * jax.nn.silu(z)).astype(BF16)
        return carry

    lax.fori_loop(0, seq // GDN_ROWS, post, 0)


def _gdn_mixer(u3, ab3, conv_w, a_log, dt_bias, norm_w):
    bsz, seq, _ = u3.shape
    gw = GW
    cwq, cwk, cwv = conv_w[:, :gw], conv_w[:, gw:2 * gw], conv_w[:, 2 * gw:]
    alog = jnp.pad(a_log, (0, LANES - GDN_HEADS))[None, :]
    dtb = jnp.pad(dt_bias, (0, LANES - GDN_HEADS))[None, :]
    nw = jnp.tile(norm_w, GDN_HEADS)[None, :]
    col = lambda c: pl.BlockSpec((None, seq, gw), lambda b: (b, 0, c // gw))
    return pl.pallas_call(
        functools.partial(_gdn_kernel, seq=seq),
        out_shape=jax.ShapeDtypeStruct((bsz, seq, gw), BF16),
        grid=(bsz,),
        in_specs=[col(COL_GDN_Q), col(COL_GDN_K), col(COL_GDN_V), col(COL_GDN_Z),
                  pl.BlockSpec((None, seq, LANES), lambda b: (b, 0, 0)),
                  _const_spec((GDN_SHORT_CONV, gw)), _const_spec((GDN_SHORT_CONV, gw)),
                  _const_spec((GDN_SHORT_CONV, gw)),
                  _const_spec((1, LANES)), _const_spec((1, LANES)), _const_spec((1, gw))],
        out_specs=pl.BlockSpec((None, seq, gw), lambda b: (b, 0, 0)),
        scratch_shapes=[pltpu.VMEM((GDN_PAD + seq, gw), F32),
                        pltpu.VMEM((GDN_PAD + seq, gw), F32),
                        pltpu.VMEM((GDN_PAD + seq, gw), F32),
                        pltpu.VMEM((seq, gw), F32),
                        pltpu.VMEM((seq, gw), F32),
                        pltpu.VMEM((seq, gw), F32),
                        pltpu.VMEM((seq, gw), F32),
                        pltpu.VMEM((gw, gw), F32),
                        pltpu.VMEM((gw, gw), BF16)],
        compiler_params=_cparams(("parallel",)),
        name="gdn_mixer",
    )(u3, u3, u3, u3, ab3, cwq, cwk, cwv, alog, dtb, nw)


def _alibi_slopes(n):
    start = 2.0 ** (-8.0 / n)
    return np.array([start ** (i + 1) for i in range(n)], dtype=np.float32)


def _alibi_tables(seq):
    slopes = _alibi_slopes(DIFF_HEADS).astype(np.float64)
    t = np.arange(seq, dtype=np.int64)
    lo = (t % 256).astype(np.float64)
    hi = (t - t % 256).astype(np.float64)
    posq = np.zeros((DIFF_HEADS, seq, LANES), np.float32)
    posk = np.zeros((DIFF_HEADS, seq, LANES), np.float32)
    for h in range(DIFF_HEADS):
        posq[h, :, 0] = -slopes[h] * hi
        posq[h, :, 1] = -slopes[h] * lo
        posq[h, :, 2] = 1.0
        posq[h, :, 3] = 1.0
        posk[h, :, 0] = 1.0
        posk[h, :, 1] = 1.0
        posk[h, :, 2] = slopes[h] * hi
        posk[h, :, 3] = slopes[h] * lo
    pq, pk = posq.astype(BF16), posk.astype(BF16)
    assert np.array_equal(pq.astype(np.float32), posq), "ALiBi factors must be bf16-exact"
    assert np.array_equal(pk.astype(np.float32), posk), "ALiBi factors must be bf16-exact"
    return jnp.asarray(pq), jnp.asarray(pk)


def _attn_kernel(q_ref, k_ref, v_ref, pq_ref, pk_ref, lam_ref, w_ref, y_ref,
                 ka_ref, va_ref, m_ref, acc_ref, sa_ref, sb_ref, *, tq, lam_init):
    dv = DIFF_V_DIM
    seq = q_ref.shape[0]
    nq = seq // tq

    ka_ref[:, 0:LANES] = k_ref[...]
    ka_ref[:, LANES:2 * LANES] = pk_ref[...]
    va_ref[:, 0:dv] = v_ref[...]
    va_ref[:, dv:2 * dv] = jnp.ones(v_ref.shape, BF16)

    lv = lam_ref[...]
    lam = (jnp.exp(jnp.sum(lv[0:1, :] * lv[1:2, :], axis=-1, keepdims=True))
           - jnp.exp(jnp.sum(lv[2:3, :] * lv[3:4, :], axis=-1, keepdims=True)) + lam_init)
    scale = jnp.asarray(DIFF_QK_DIM ** -0.5, BF16)

    def scores(i, j):
        q = q_ref[i * tq:(i + 1) * tq, :]
        lane = lax.broadcasted_iota(jnp.int32, q.shape, 1)
        zero = jnp.zeros_like(q)
        q1 = jnp.where(lane < DIFF_QK_DIM, q, zero) * scale
        q2 = jnp.where(lane >= DIFF_QK_DIM, q, zero) * scale
        pq = pq_ref[i * tq:(i + 1) * tq, :]
        qa = jnp.concatenate([jnp.concatenate([q1, pq], axis=1),
                              jnp.concatenate([q2, pq], axis=1)], axis=0)
        return _dot_nt(qa, ka_ref[j * tq:(j + 1) * tq, :])

    def softmax_pv(s, i, j):
        if j == i:
            row = lax.broadcasted_iota(jnp.int32, s.shape, 0) % tq
            col = lax.broadcasted_iota(jnp.int32, s.shape, 1)
            s = jnp.where(col <= row, s, -jnp.inf)
        m_cur = jnp.max(s, axis=1, keepdims=True)
        if j == 0:
            m_new = jnp.broadcast_to(m_cur, m_ref.shape)
        else:
            m_prev = m_ref[...]
            m_new = jnp.maximum(m_prev, m_cur)
        p = jnp.exp(s - pltpu.repeat(m_new, tq // LANES, axis=1)).astype(BF16)
        acc = _dot(p, va_ref[j * tq:(j + 1) * tq, :])
        if j > 0:
            alpha = jnp.exp(m_prev - m_new)
            acc = acc_ref[...] * pltpu.repeat(alpha, 2 * dv // LANES, axis=1) + acc
        if j < i:
            acc_ref[...] = acc
            m_ref[...] = m_new
        else:
            o1 = acc[0:tq, 0:dv] / acc[0:tq, dv:2 * dv]
            o2 = acc[tq:2 * tq, 0:dv] / acc[tq:2 * tq, dv:2 * dv]
            o = o1 - lam * o2
            y_ref[i * tq:(i + 1) * tq, :] = (_rms(o, w_ref[...]) * (1.0 - lam_init)).astype(BF16)

    tiles = [(i, j) for i in range(nq) for j in range(i + 1)]
    bufs = (sa_ref, sb_ref)
    bufs[0][...] = scores(*tiles[0])
    for t, (i, j) in enumerate(tiles):
        if t + 1 < len(tiles):
            bufs[(t + 1) % 2][...] = scores(*tiles[t + 1])
        softmax_pv(bufs[t % 2][...], i, j)


def _diff_attention(u3, lam_vecs, subln_w, lam_init, tq):
    bsz, seq, _ = u3.shape
    posq, posk = _alibi_tables(seq)
    dv = DIFF_V_DIM
    col = lambda c: pl.BlockSpec((None, seq, dv), lambda b, h: (b, 0, c // dv + h))
    pos = pl.BlockSpec((None, seq, LANES), lambda b, h: (h, 0, 0))
    return pl.pallas_call(
        functools.partial(_attn_kernel, tq=tq, lam_init=lam_init),
        out_shape=jax.ShapeDtypeStruct((bsz, seq, DIFF_WIDTH), BF16),
        grid=(bsz, DIFF_HEADS),
        in_specs=[col(COL_DIFF_Q), col(COL_DIFF_K), col(COL_DIFF_V), pos, pos,
                  pl.BlockSpec((4, DIFF_QK_DIM), lambda b, h: (0, 0)),
                  pl.BlockSpec((1, dv), lambda b, h: (0, 0))],
        out_specs=pl.BlockSpec((None, seq, dv), lambda b, h: (b, 0, h)),
        scratch_shapes=[pltpu.VMEM((seq, 2 * LANES), BF16),
                        pltpu.VMEM((seq, 2 * dv), BF16),
                        pltpu.VMEM((2 * tq, LANES), F32),
                        pltpu.VMEM((2 * tq, 2 * dv), F32),
                        pltpu.VMEM((2 * tq, tq), F32),
                        pltpu.VMEM((2 * tq, tq), F32)],
        compiler_params=_cparams(("parallel", "parallel")),
        name="diff_attention",
    )(u3, u3, u3, posq, posk, lam_vecs, subln_w)


def _out_ffn_kernel(x_ref, yc_ref, yg_ref, yd_ref, wo_ref, g2_ref, wi_ref, wf_ref, gf_ref, o_ref,
                    *, d_ff, final):
    c0, c1 = CONV_WIDTH, CONV_WIDTH + GDN_WIDTH
    x1 = (x_ref[...]
          + _dot(yc_ref[...], wo_ref[0:c0, :])
          + _dot(yg_ref[...], wo_ref[c0:c1, :])
          + _dot(yd_ref[...], wo_ref[c1:, :]))
    h = _rms(x1, g2_ref[...]).astype(BF16)
    gate = _dot(h, wi_ref[:, 0:d_ff])
    up = _dot(h, wi_ref[:, d_ff:2 * d_ff])
    a = (jax.nn.silu(gate) * up).astype(BF16)
    out = x1 + _dot(a, wf_ref[...])
    if final:
        out = _rms(out, gf_ref[...])
    o_ref[...] = out


def _out_ffn(x2, y_conv, y_gdn, y_diff, w_out, g2, w_ffn_in, w_ffn_out, g_final, final, tm):
    m, d = x2.shape
    d_ff = w_ffn_out.shape[0]
    row = lambda w: pl.BlockSpec((tm, w), lambda i: (i, 0))
    return pl.pallas_call(
        functools.partial(_out_ffn_kernel, d_ff=d_ff, final=final),
        out_shape=jax.ShapeDtypeStruct((m, d), F32),
        grid=(m // tm,),
        in_specs=[row(d), row(CONV_WIDTH), row(GDN_WIDTH), row(DIFF_WIDTH),
                  _const_spec(w_out.shape), _const_spec((1, d)),
                  _const_spec(w_ffn_in.shape), _const_spec(w_ffn_out.shape), _const_spec((1, d))],
        out_specs=row(d),
        compiler_params=_cparams(("parallel",)),
        name="out_ffn",
    )(x2, y_conv, y_gdn, y_diff, w_out, g2, w_ffn_in, w_ffn_out, g_final)


def _pick_tile(n, pref):
    t = min(pref, n)
    while n % t:
        t //= 2
    return t


def kernel(x, norm1_g, w_in, conv_dw_w, conv_dw_b, conv_ln_g, conv_ln_b, gdn_conv_w, gdn_a_log,
           gdn_dt_bias, gdn_norm_w, diff_lambda, diff_subln_w, w_out, norm2_g, w_ffn_in, w_ffn_out,
           final_norm_g):
    bsz, seq, d = x.shape
    depth = w_in.shape[0]
    m = bsz * seq
    assert seq % GDN_ROWS == 0 and seq % CONV_ROWS == 0
    tm = _pick_tile(m, 512)
    tq = _pick_tile(seq, 512)

    w_all = jnp.concatenate(
        [w_in[..., :AB_START], w_in[..., AB_START + AB_WIDTH:], w_in[..., AB_START:AB_START + AB_WIDTH],
         jnp.zeros(w_in.shape[:2] + (LANES - AB_WIDTH,), w_in.dtype)], axis=-1).astype(BF16)
    w_out_b = w_out.astype(BF16)
    w_ffn_in_b = w_ffn_in.astype(BF16)
    w_ffn_out_b = w_ffn_out.astype(BF16)

    x2 = x.reshape(m, d)
    for l in range(depth):
        lam_init = 0.8 - 0.6 * math.exp(-0.3 * l)
        u2, ab2 = _in_proj(x2, norm1_g[l][None, :], w_all[l], tm)
        u3 = u2.reshape(bsz, seq, MAIN_WIDTH)
        ab3 = ab2.reshape(bsz, seq, LANES)
        y_conv = _conv_mixer(u3, conv_dw_w[l], conv_dw_b[l][None, :], conv_ln_g[l][None, :],
                             conv_ln_b[l][None, :])
        y_gdn = _gdn_mixer(u3, ab3, gdn_conv_w[l], gdn_a_log[l], gdn_dt_bias[l], gdn_norm_w[l])
        y_diff = _diff_attention(u3, diff_lambda[l], diff_subln_w[l][None, :], lam_init, tq)
        x2 = _out_ffn(x2, y_conv.reshape(m, CONV_WIDTH), y_gdn.reshape(m, GDN_WIDTH),
                      y_diff.reshape(m, DIFF_WIDTH), w_out_b[l], norm2_g[l][None, :],
                      w_ffn_in_b[l], w_ffn_out_b[l], final_norm_g[None, :],
                      final=(l == depth - 1), tm=tm)
    return x2.reshape(bsz, seq, d)
```
